```python
import math
import jax, jax.numpy as jnp
from jax import lax
import numpy as np

D_MODEL = 1024
BATCH = 8
SEQ = 4096
DEPTH = 1

MIX_WIDTH = D_MODEL
DA_WIDTH = MIX_WIDTH // 2
ML_WIDTH = MIX_WIDTH - DA_WIDTH
DA_HEADS = 4
DA_VDIM = DA_WIDTH // DA_HEADS
DA_QKDIM = DA_VDIM // 2
DA_QK_COLS = DA_HEADS * 2 * DA_QKDIM
ML_HEADS = 4
ML_DIM = ML_WIDTH // ML_HEADS
ML_CHUNK = 64
ML_CONV = 4
FFN_CONV = 3
D_FF = 2816
Q_BLOCK = 128
NORM_EPS = 1e-6
IN_SIZES = (DA_QK_COLS, DA_QK_COLS, DA_WIDTH, 2 * ML_WIDTH, ML_WIDTH, ML_WIDTH, ML_HEADS, ML_HEADS)
IN_COLS = DA_QK_COLS * 2 + DA_WIDTH + 4 * ML_WIDTH + 2 * ML_HEADS

kernel_name = "hybrid_diffattn_mlstm_convglu"


def rmsnorm(x, w):
    x32 = x.astype(jnp.float32)
    y = x32 * lax.rsqrt(jnp.mean(x32 * x32, axis=-1, keepdims=True) + NORM_EPS)
    return (y * w.astype(jnp.float32)).astype(x.dtype)


def causal_dwconv(x, w, b):
    width, ch = w.shape
    y = lax.conv_general_dilated(
        x, w[:, None, :].astype(x.dtype), window_strides=(1,),
        padding=[(width - 1, 0)], dimension_numbers=('NWC', 'WIO', 'NWC'),
        feature_group_count=ch)
    return y + b.astype(x.dtype)


def alibi_slopes(n_heads):
    return 2.0 ** (-8.0 * jnp.arange(1, n_heads + 1, dtype=jnp.float32) / n_heads)


def split_projection(proj):
    idx = np.cumsum(IN_SIZES)[:-1].tolist()
    return jnp.split(proj, idx, axis=-1)


def diff_attention(q, k, v, lam, slopes):
    B, S, H = q.shape[0], q.shape[1], q.shape[2]
    nb = S // Q_BLOCK
    scale = DA_QKDIM ** -0.5
    q_blocks = q.reshape(B, nb, Q_BLOCK, H, 2, DA_QKDIM).swapaxes(0, 1)
    kpos = jnp.arange(S)

    def one_block(args):
        qb, blk = args
        qpos = blk * Q_BLOCK + jnp.arange(Q_BLOCK)
        s = jnp.einsum('bqhcd,bshcd->bhcqs', qb, k, preferred_element_type=jnp.float32) * scale
        dist = (qpos[:, None] - kpos[None, :]).astype(jnp.float32)
        bias = -slopes[:, None, None, None] * dist
        s = jnp.where(dist >= 0, s + bias, -jnp.inf)
        p = jax.nn.softmax(s, axis=-1)
        a = p[:, :, 0] - lam * p[:, :, 1]
        return jnp.einsum('bhqs,bshd->bqhd', a.astype(v.dtype), v)

    out = lax.map(one_block, (q_blocks, jnp.arange(nb)))
    return out.swapaxes(0, 1).reshape(B, S, H, v.shape[-1])


def mlstm_chunkwise(q, k, v, i_pre, logf):
    B, S, H, d = q.shape
    L = ML_CHUNK
    nc = S // L

    def to_chunks(t):
        t = t.astype(jnp.float32).reshape((B, nc, L) + t.shape[2:])
        return jnp.moveaxis(t.swapaxes(0, 1), 3, 2)

    qc, kc, vc, ic, fc = (to_chunks(t) for t in (q, k, v, i_pre, logf))
    causal = jnp.tril(jnp.ones((L, L), dtype=bool))

    def step(carry, xs):
        C, n, m = carry
        qb, kb, vb, ib, fb = xs
        b = jnp.cumsum(fb, axis=-1)
        log_d = b[..., :, None] - b[..., None, :] + ib[..., None, :]
        log_d = jnp.where(causal, log_d, -jnp.inf)
        log_prev = b + m[..., None]
        m_t = jnp.maximum(log_prev, jnp.max(log_d, axis=-1))
        d_mat = jnp.exp(log_d - m_t[..., None])
        w_prev = jnp.exp(log_prev - m_t)
        sc = jnp.einsum('bhld,bhsd->bhls', qb, kb) * d_mat
        num = jnp.einsum('bhls,bhsd->bhld', sc, vb) + w_prev[..., None] * jnp.einsum('bhlk,bhkv->bhlv', qb, C)
        den = jnp.sum(sc, axis=-1) + w_prev * jnp.einsum('bhlk,bhk->bhl', qb, n)
        h = num / jnp.maximum(jnp.abs(den), jnp.exp(-m_t))[..., None]
        w_state = d_mat[..., -1, :]
        w_last = w_prev[..., -1]
        C_new = w_last[..., None, None] * C + jnp.einsum('bhs,bhsk,bhsv->bhkv', w_state, kb, vb)
        n_new = w_last[..., None] * n + jnp.einsum('bhs,bhsk->bhk', w_state, kb)
        return (C_new, n_new, m_t[..., -1]), h

    init = (jnp.zeros((B, H, d, d), jnp.float32), jnp.zeros((B, H, d), jnp.float32),
            jnp.zeros((B, H), jnp.float32))
    _, hs = lax.scan(step, init, (qc, kc, vc, ic, fc))
    hs = jnp.moveaxis(hs, 2, 3).swapaxes(0, 1).reshape(B, S, H, d)
    return hs.astype(q.dtype)


def setup_inputs(seed: int = 0) -> dict:
    key = jax.random.key(seed)
    ks = jax.random.split(key, 20)
    f32 = jnp.float32

    def nrm(k, shape, s):
        return jax.random.normal(k, shape, f32) * s

    return {
        "x": nrm(ks[0], (BATCH, SEQ, D_MODEL), 1.0),
        "attn_norm_w": 1.0 + nrm(ks[1], (DEPTH, D_MODEL), 0.02),
        "w_in": nrm(ks[2], (DEPTH, D_MODEL, IN_COLS), D_MODEL ** -0.5),
        "mlstm_conv_w": nrm(ks[3], (DEPTH, ML_CONV, 2 * ML_WIDTH), ML_CONV ** -0.5),
        "mlstm_conv_b": nrm(ks[4], (DEPTH, 2 * ML_WIDTH), 0.01),
        "mlstm_igate_b": nrm(ks[5], (DEPTH, ML_HEADS), 0.1),
        "mlstm_fgate_b": jnp.linspace(3.0, 6.0, ML_HEADS, dtype=f32)[None, :] + nrm(ks[6], (DEPTH, ML_HEADS), 0.01),
        "lambda_q1": nrm(ks[7], (DEPTH, DA_QKDIM), 0.1),
        "lambda_k1": nrm(ks[8], (DEPTH, DA_QKDIM), 0.1),
        "lambda_q2": nrm(ks[9], (DEPTH, DA_QKDIM), 0.1),
        "lambda_k2": nrm(ks[10], (DEPTH, DA_QKDIM), 0.1),
        "diff_norm_w": 1.0 + nrm(ks[11], (DEPTH, DA_VDIM), 0.02),
        "mlstm_norm_w": 1.0 + nrm(ks[12], (DEPTH, ML_WIDTH), 0.02),
        "w_out": nrm(ks[13], (DEPTH, MIX_WIDTH, D_MODEL), MIX_WIDTH ** -0.5),
        "ffn_norm_w": 1.0 + nrm(ks[14], (DEPTH, D_MODEL), 0.02),
        "w_up": nrm(ks[15], (DEPTH, D_MODEL, 2 * D_FF), D_MODEL ** -0.5),
        "ffn_conv_w": nrm(ks[16], (DEPTH, FFN_CONV, 2 * D_FF), FFN_CONV ** -0.5),
        "ffn_conv_b": nrm(ks[17], (DEPTH, 2 * D_FF), 0.01),
        "w_down": nrm(ks[18], (DEPTH, D_FF, D_MODEL), D_FF ** -0.5),
        "final_norm_w": 1.0 + nrm(ks[19], (D_MODEL,), 0.02),
    }


def reference(x, attn_norm_w, w_in, mlstm_conv_w, mlstm_conv_b, mlstm_igate_b, mlstm_fgate_b,
              lambda_q1, lambda_k1, lambda_q2, lambda_k2, diff_norm_w, mlstm_norm_w, w_out,
              ffn_norm_w, w_up, ffn_conv_w, ffn_conv_b, w_down, final_norm_w):
    B, S, _ = x.shape
    slopes = alibi_slopes(DA_HEADS)
    for l in range(DEPTH):
        lam_init = 0.8 - 0.6 * math.exp(-0.3 * l)
        hn = rmsnorm(x, attn_norm_w[l])
        proj = hn @ w_in[l]
        da_q, da_k, da_v, ml_qk, ml_v, ml_o, ml_i, ml_f = split_projection(proj)

        lam = (jnp.exp(jnp.sum(lambda_q1[l].astype(jnp.float32) * lambda_k1[l].astype(jnp.float32)))
               - jnp.exp(jnp.sum(lambda_q2[l].astype(jnp.float32) * lambda_k2[l].astype(jnp.float32)))
               + lam_init)
        da_out = diff_attention(da_q.reshape(B, S, DA_HEADS, 2, DA_QKDIM),
                                da_k.reshape(B, S, DA_HEADS, 2, DA_QKDIM),
                                da_v.reshape(B, S, DA_HEADS, DA_VDIM), lam, slopes)
        da_out = rmsnorm(da_out, diff_norm_w[l]) * (1.0 - lam_init)
        da_out = da_out.reshape(B, S, DA_WIDTH)

        qk = jax.nn.silu(causal_dwconv(ml_qk, mlstm_conv_w[l], mlstm_conv_b[l]))
        mq, mk = jnp.split(qk, 2, axis=-1)
        mq = mq.reshape(B, S, ML_HEADS, ML_DIM)
        mk = mk.reshape(B, S, ML_HEADS, ML_DIM) * (ML_DIM ** -0.5)
        mv = ml_v.reshape(B, S, ML_HEADS, ML_DIM)
        i_pre = ml_i.astype(jnp.float32) + mlstm_igate_b[l].astype(jnp.float32)
        logf = jax.nn.log_sigmoid(ml_f.astype(jnp.float32) + mlstm_fgate_b[l].astype(jnp.float32))
        h_tilde = mlstm_chunkwise(mq, mk, mv, i_pre, logf)
        h_tilde = rmsnorm(h_tilde, mlstm_norm_w[l].reshape(ML_HEADS, ML_DIM))
        ml_out = jax.nn.sigmoid(ml_o) * h_tilde.reshape(B, S, ML_WIDTH)

        mixed = jnp.concatenate([da_out, ml_out], axis=-1)
        x = x + mixed @ w_out[l]

        hn = rmsnorm(x, ffn_norm_w[l])
        u = causal_dwconv(hn @ w_up[l], ffn_conv_w[l], ffn_conv_b[l])
        gate, up = jnp.split(u, 2, axis=-1)
        x = x + (jax.nn.silu(gate) * up) @ w_down[l]
    return rmsnorm(x, final_norm_w)
```

```python
import functools
import math

import jax
import jax.numpy as jnp
from jax import lax
from jax.experimental import pallas as pl
from jax.experimental.pallas import tpu as pltpu

F32 = jnp.float32
BF16 = jnp.bfloat16

D_MODEL = 1024
DA_HEADS = 4
DA_VDIM = 128
DA_QKDIM = 64
DA_WIDTH = DA_HEADS * DA_VDIM
ML_HEADS = 4
ML_DIM = 128
ML_WIDTH = ML_HEADS * ML_DIM
ML_CHUNK = 64
ML_CONV = 4
FFN_CONV = 3
D_FF = 2816
NORM_EPS = 1e-6
LAM_INIT = 0.8 - 0.6 * math.exp(-0.3 * 0)
N_GATES = 2 * ML_HEADS

LANES = 128
HALO = 16
NEG_BIG = -1e30
VMEM_LIMIT = 56 * 1024 * 1024

MAIN_COLS = 2 * DA_WIDTH + DA_WIDTH + 2 * ML_WIDTH + ML_WIDTH + ML_WIDTH
PROJ_COLS = MAIN_COLS + LANES


def _params(*sem):
    return pltpu.CompilerParams(dimension_semantics=sem, vmem_limit_bytes=VMEM_LIMIT)


def _rms(x, w):
    return x * lax.rsqrt(jnp.mean(x * x, axis=-1, keepdims=True) + NORM_EPS) * w


def _inproj_kernel(x_ref, nw_ref, w_ref, q_ref, k_ref, v_ref, mqk_ref, mv_ref, mo_ref, g_ref):
    hn = _rms(x_ref[...], nw_ref[...]).astype(BF16)

    def mm(lo, hi):
        return jnp.dot(hn, w_ref[:, lo:hi], preferred_element_type=F32)

    c = 0
    q_ref[...] = (mm(c, c + 512) * (DA_QKDIM ** -0.5)).astype(BF16); c += 512
    k_ref[...] = mm(c, c + 512).astype(BF16); c += 512
    v_ref[...] = mm(c, c + 512).astype(BF16); c += 512
    mqk_ref[:, :512] = mm(c, c + 512).astype(BF16); c += 512
    mqk_ref[:, 512:] = mm(c, c + 512).astype(BF16); c += 512
    mv_ref[...] = mm(c, c + 512).astype(BF16); c += 512
    mo_ref[...] = mm(c, c + 512).astype(BF16); c += 512
    g_ref[...] = mm(c, c + LANES)[:, :N_GATES]


def _inproj(x2, nw, w, tm):
    T = x2.shape[0]
    row = lambda i: (i, 0)
    const = lambda i: (0, 0)
    bf = lambda n: jax.ShapeDtypeStruct((T, n), BF16)
    return pl.pallas_call(
        _inproj_kernel,
        grid=(T // tm,),
        in_specs=[pl.BlockSpec((tm, D_MODEL), row),
                  pl.BlockSpec((1, D_MODEL), const),
                  pl.BlockSpec((D_MODEL, PROJ_COLS), const)],
        out_specs=[pl.BlockSpec((tm, 512), row), pl.BlockSpec((tm, 512), row),
                   pl.BlockSpec((tm, 512), row), pl.BlockSpec((tm, 1024), row),
                   pl.BlockSpec((tm, 512), row), pl.BlockSpec((tm, 512), row),
                   pl.BlockSpec((tm, N_GATES), row)],
        out_shape=[bf(512), bf(512), bf(512), bf(1024), bf(512), bf(512),
                   jax.ShapeDtypeStruct((T, N_GATES), F32)],
        compiler_params=_params("parallel"),
        name="inproj",
    )(x2, nw, w)


def _attn_kernel(q_ref, k_ref, v_ref, lam_ref, nw_ref, o_ref, m_sc, l_sc, acc_sc, *, tq):
    qi = pl.program_id(1)
    ki = pl.program_id(2)

    @pl.when(ki == 0)
    def _():
        m_sc[...] = jnp.full(m_sc.shape, NEG_BIG, F32)
        l_sc[...] = jnp.zeros(l_sc.shape, F32)
        acc_sc[...] = jnp.zeros(acc_sc.shape, F32)

    @pl.when(ki <= qi)
    def _():
        row = lax.broadcasted_iota(jnp.int32, (tq, tq), 0)
        col = lax.broadcasted_iota(jnp.int32, (tq, tq), 1)
        rel = (col - row + (ki - qi) * tq).astype(F32)
        keep = rel <= 0.0
        lane = lax.broadcasted_iota(jnp.int32, (tq, DA_VDIM), 1)
        for h in range(DA_HEADS):
            slope = 2.0 ** (-8.0 * (h + 1) / DA_HEADS)
            bias = jnp.where(keep, rel * slope, NEG_BIG)
            hs = slice(h * DA_VDIM, (h + 1) * DA_VDIM)
            qh = q_ref[:, hs]
            kh = k_ref[:, hs]
            vh = v_ref[:, hs]
            for c in range(2):
                qc = jnp.where((lane >= c * DA_QKDIM) & (lane < (c + 1) * DA_QKDIM), qh, jnp.zeros_like(qh))
                s = lax.dot_general(qc, kh, (((1,), (1,)), ((), ())), preferred_element_type=F32) + bias
                i = 2 * h + c
                m_prev = m_sc[i]
                m_new = jnp.maximum(m_prev, jnp.max(s, axis=-1, keepdims=True))
                alpha = jnp.exp(m_prev - m_new)
                p = jnp.exp(s - m_new)
                l_sc[i] = alpha * l_sc[i] + jnp.sum(p, axis=-1, keepdims=True)
                acc_sc[i] = alpha * acc_sc[i] + jnp.dot(p.astype(BF16), vh, preferred_element_type=F32)
                m_sc[i] = m_new

    @pl.when(ki == qi)
    def _():
        lv = lam_ref[...]
        lam = (jnp.exp(jnp.sum(lv[0:1] * lv[1:2], axis=-1, keepdims=True))
               - jnp.exp(jnp.sum(lv[2:3] * lv[3:4], axis=-1, keepdims=True)) + LAM_INIT)
        for h in range(DA_HEADS):
            o = acc_sc[2 * h] / l_sc[2 * h] - lam * (acc_sc[2 * h + 1] / l_sc[2 * h + 1])
            o = _rms(o, nw_ref[...]) * (1.0 - LAM_INIT)
            o_ref[:, h * DA_VDIM:(h + 1) * DA_VDIM] = o.astype(BF16)


def _attn(q, k, v, lam_vecs, nw, B, S, tq):
    nq = S // tq
    qmap = lambda b, i, j: (b * nq + i, 0)
    kmap = lambda b, i, j: (b * nq + jnp.minimum(j, i), 0)
    const = lambda b, i, j: (0, 0)
    return pl.pallas_call(
        functools.partial(_attn_kernel, tq=tq),
        grid=(B, nq, nq),
        in_specs=[pl.BlockSpec((tq, DA_WIDTH), qmap),
                  pl.BlockSpec((tq, DA_WIDTH), kmap),
                  pl.BlockSpec((tq, DA_WIDTH), kmap),
                  pl.BlockSpec((4, DA_QKDIM), const),
                  pl.BlockSpec((1, DA_VDIM), const)],
        out_specs=pl.BlockSpec((tq, DA_WIDTH), qmap),
        out_shape=jax.ShapeDtypeStruct((B * S, DA_WIDTH), BF16),
        scratch_shapes=[pltpu.VMEM((2 * DA_HEADS, tq, 1), F32),
                        pltpu.VMEM((2 * DA_HEADS, tq, 1), F32),
                        pltpu.VMEM((2 * DA_HEADS, tq, DA_VDIM), F32)],
        compiler_params=_params("parallel", "parallel", "arbitrary"),
        name="attn",
    )(q, k, v, lam_vecs, nw)


def _log_sigmoid(x):
    return jnp.minimum(x, 0.0) - jnp.log(1.0 + jnp.exp(-jnp.abs(x)))


def _mlstm_kernel(qk_ref, halo_ref, v_ref, o_ref, g_ref, gt_ref, cw_ref, cb_ref, gbr_ref, gbc_ref,
                  nw_ref, out_ref, xx_sc, qk_sc, c_sc, m_sc, *, ts):
    j = pl.program_id(1)
    L = ML_CHUNK

    @pl.when(j == 0)
    def _():
        c_sc[...] = jnp.zeros(c_sc.shape, F32)
        m_sc[...] = jnp.zeros(m_sc.shape, F32)
        xx_sc[0:HALO, :] = jnp.zeros((HALO, 2 * ML_WIDTH), F32)

    @pl.when(j > 0)
    def _():
        xx_sc[0:HALO, :] = halo_ref[...].astype(F32)

    xx_sc[HALO:, :] = qk_ref[...].astype(F32)
    y = cb_ref[...]
    for t in range(ML_CONV):
        off = HALO - (ML_CONV - 1) + t
        y = y + cw_ref[t:t + 1, :] * xx_sc[off:off + ts, :]
    y = y / (1.0 + jnp.exp(-y))
    qk_sc[:, :ML_WIDTH] = y[:, :ML_WIDTH].astype(BF16)
    qk_sc[:, ML_WIDTH:] = (y[:, ML_WIDTH:] * (ML_DIM ** -0.5)).astype(BF16)

    tt = lax.broadcasted_iota(jnp.int32, (L, L), 0)
    ss = lax.broadcasted_iota(jnp.int32, (L, L), 1)
    causal = ss <= tt
    lane = lax.broadcasted_iota(jnp.int32, (L, ML_DIM), 1)
    ones_col = jnp.where(lane == 0, 1.0, 0.0).astype(F32)

    def chunk(ci, carry):
        r0 = pl.multiple_of(ci * L, L)
        g = g_ref[ci] + gbr_ref[...]
        gt = gt_ref[ci] + gbc_ref[...]
        for h in range(ML_HEADS):
            i_col = g[:, h:h + 1]
            f_col = _log_sigmoid(g[:, ML_HEADS + h:ML_HEADS + h + 1])
            i_row = gt[h:h + 1, :]
            f_row = _log_sigmoid(gt[ML_HEADS + h:ML_HEADS + h + 1, :])
            b_col = jnp.sum(jnp.where(causal, f_row, 0.0), axis=1, keepdims=True)
            b_row = jnp.sum(jnp.where(tt <= ss, f_col, 0.0), axis=0, keepdims=True)
            log_d = jnp.where(causal, b_col - b_row + i_row, NEG_BIG)
            m_old = m_sc[h]
            log_prev = b_col + m_old
            m_t = jnp.maximum(log_prev, jnp.max(log_d, axis=1, keepdims=True))
            d_mat = jnp.exp(log_d - m_t)
            w_prev = jnp.exp(log_prev - m_t)

            hs = slice(h * ML_DIM, (h + 1) * ML_DIM)
            qh = qk_sc[pl.ds(r0, L), h * ML_DIM:(h + 1) * ML_DIM]
            kh = qk_sc[pl.ds(r0, L), ML_WIDTH + h * ML_DIM:ML_WIDTH + (h + 1) * ML_DIM]
            vh = v_ref[pl.ds(r0, L), hs].astype(F32)
            v_aug = jnp.concatenate([vh, ones_col], axis=1)

            sc = lax.dot_general(qh, kh, (((1,), (1,)), ((), ())), preferred_element_type=F32) * d_mat
            c_old = c_sc[h]
            inter = jnp.dot(qh, c_old.astype(BF16), preferred_element_type=F32)
            intra = jnp.dot(sc.astype(BF16), v_aug.astype(BF16), preferred_element_type=F32)
            tot = intra + w_prev * inter
            num = tot[:, :ML_DIM]
            den = tot[:, ML_DIM:ML_DIM + 1]
            hh = num / jnp.maximum(jnp.abs(den), jnp.exp(-m_t))

            b_last = b_col[L - 1:L, :]
            m_last = m_t[L - 1:L, :]
            w_state = jnp.exp(b_last - b_col + i_col - m_last)
            w_last = w_prev[L - 1:L, :]
            upd = lax.dot_general(kh, (w_state * v_aug).astype(BF16), (((0,), (0,)), ((), ())),
                                  preferred_element_type=F32)
            c_sc[h] = w_last * c_old + upd
            m_sc[h] = m_last

            hh = _rms(hh, nw_ref[:, hs])
            gate = o_ref[pl.ds(r0, L), hs].astype(F32)
            out_ref[pl.ds(r0, L), hs] = (hh / (1.0 + jnp.exp(-gate))).astype(BF16)
        return carry

    lax.fori_loop(0, ts // L, chunk, 0)


def _mlstm(mqk, mv, mo, g3, gt3, cw, cb, gb_row, gb_col, nw, B, S, ts):
    ns = S // ts
    cpb = ts // ML_CHUNK
    row = lambda b, j: (b * ns + j, 0)
    halo = lambda b, j: (jnp.maximum((b * S + j * ts) // HALO - 1, 0), 0)
    ch = lambda b, j: (b * ns + j, 0, 0)
    const = lambda b, j: (0, 0)
    return pl.pallas_call(
        functools.partial(_mlstm_kernel, ts=ts),
        grid=(B, ns),
        in_specs=[pl.BlockSpec((ts, 2 * ML_WIDTH), row),
                  pl.BlockSpec((HALO, 2 * ML_WIDTH), halo),
                  pl.BlockSpec((ts, ML_WIDTH), row),
                  pl.BlockSpec((ts, ML_WIDTH), row),
                  pl.BlockSpec((cpb, ML_CHUNK, N_GATES), ch),
                  pl.BlockSpec((cpb, N_GATES, ML_CHUNK), ch),
                  pl.BlockSpec((ML_CONV, 2 * ML_WIDTH), const),
                  pl.BlockSpec((1, 2 * ML_WIDTH), const),
                  pl.BlockSpec((1, N_GATES), const),
                  pl.BlockSpec((N_GATES, 1), const),
                  pl.BlockSpec((1, ML_WIDTH), const)],
        out_specs=pl.BlockSpec((ts, ML_WIDTH), row),
        out_shape=jax.ShapeDtypeStruct((B * S, ML_WIDTH), BF16),
        scratch_shapes=[pltpu.VMEM((ts + HALO, 2 * ML_WIDTH), F32),
                        pltpu.VMEM((ts, 2 * ML_WIDTH), BF16),
                        pltpu.VMEM((ML_HEADS, ML_DIM, 2 * ML_DIM), F32),
                        pltpu.VMEM((ML_HEADS, 1, 1), F32)],
        compiler_params=_params("parallel", "arbitrary"),
        name="mlstm",
    )(mqk, mqk, mv, mo, g3, gt3, cw, cb, gb_row, gb_col, nw)


def _outproj_kernel(x_ref, da_ref, ml_ref, wa_ref, wm_ref, o_ref):
    o_ref[...] = (x_ref[...]
                  + jnp.dot(da_ref[...], wa_ref[...], preferred_element_type=F32)
                  + jnp.dot(ml_ref[...], wm_ref[...], preferred_element_type=F32))


def _outproj(x2, da, ml, wa, wm, tm):
    T = x2.shape[0]
    row = lambda i: (i, 0)
    const = lambda i: (0, 0)
    return pl.pallas_call(
        _outproj_kernel,
        grid=(T // tm,),
        in_specs=[pl.BlockSpec((tm, D_MODEL), row),
                  pl.BlockSpec((tm, DA_WIDTH), row),
                  pl.BlockSpec((tm, ML_WIDTH), row),
                  pl.BlockSpec((DA_WIDTH, D_MODEL), const),
                  pl.BlockSpec((ML_WIDTH, D_MODEL), const)],
        out_specs=pl.BlockSpec((tm, D_MODEL), row),
        out_shape=jax.ShapeDtypeStruct((T, D_MODEL), F32),
        compiler_params=_params("parallel"),
        name="outproj",
    )(x2, da, ml, wa, wm)


def _ffn_kernel(x_ref, halo_ref, nw_ref, wg_ref, wu_ref, cwg_ref, cwu_ref, cbg_ref, cbu_ref, wd_ref,
                fw_ref, o_ref, hn_sc, ug_sc, uu_sc, acc_sc, *, tm, tiles_per_seq):
    i = pl.program_id(0)
    f = pl.program_id(1)

    @pl.when(f == 0)
    def _():
        hn_sc[HALO:, :] = _rms(x_ref[...], nw_ref[...]).astype(BF16)
        acc_sc[...] = jnp.zeros(acc_sc.shape, F32)

    @pl.when((f == 0) & (i % tiles_per_seq == 0))
    def _():
        hn_sc[0:HALO, :] = jnp.zeros((HALO, D_MODEL), BF16)

    @pl.when((f == 0) & (i % tiles_per_seq != 0))
    def _():
        hn_sc[0:HALO, :] = _rms(halo_ref[...], nw_ref[...]).astype(BF16)

    hn = hn_sc[...]
    ug_sc[...] = jnp.dot(hn, wg_ref[...], preferred_element_type=F32)
    uu_sc[...] = jnp.dot(hn, wu_ref[...], preferred_element_type=F32)
    yg = cbg_ref[...]
    yu = cbu_ref[...]
    for t in range(FFN_CONV):
        off = HALO - (FFN_CONV - 1) + t
        yg = yg + cwg_ref[t:t + 1, :] * ug_sc[off:off + tm, :]
        yu = yu + cwu_ref[t:t + 1, :] * uu_sc[off:off + tm, :]
    act = (yg / (1.0 + jnp.exp(-yg))) * yu
    acc_sc[...] += jnp.dot(act.astype(BF16), wd_ref[...], preferred_element_type=F32)

    @pl.when(f == pl.num_programs(1) - 1)
    def _():
        o_ref[...] = _rms(x_ref[...] + acc_sc[...], fw_ref[...])


def _ffn(x1, nw, w_up, cw, cb, w_down, fw, S, tm, fc):
    T = x1.shape[0]
    nf = D_FF // fc
    row = lambda i, f: (i, 0)
    halo = lambda i, f: (jnp.maximum(i * (tm // HALO) - 1, 0), 0)
    const = lambda i, f: (0, 0)
    gcol = lambda i, f: (0, f)
    ucol = lambda i, f: (0, nf + f)
    return pl.pallas_call(
        functools.partial(_ffn_kernel, tm=tm, tiles_per_seq=S // tm),
        grid=(T // tm, nf),
        in_specs=[pl.BlockSpec((tm, D_MODEL), row),
                  pl.BlockSpec((HALO, D_MODEL), halo),
                  pl.BlockSpec((1, D_MODEL), const),
                  pl.BlockSpec((D_MODEL, fc), gcol),
                  pl.BlockSpec((D_MODEL, fc), ucol),
                  pl.BlockSpec((FFN_CONV, fc), gcol),
                  pl.BlockSpec((FFN_CONV, fc), ucol),
                  pl.BlockSpec((1, fc), gcol),
                  pl.BlockSpec((1, fc), ucol),
                  pl.BlockSpec((fc, D_MODEL), lambda i, f: (f, 0)),
                  pl.BlockSpec((1, D_MODEL), const)],
        out_specs=pl.BlockSpec((tm, D_MODEL), row),
        out_shape=jax.ShapeDtypeStruct((T, D_MODEL), F32),
        scratch_shapes=[pltpu.VMEM((tm + HALO, D_MODEL), BF16),
                        pltpu.VMEM((tm + HALO, fc), F32),
                        pltpu.VMEM((tm + HALO, fc), F32),
                        pltpu.VMEM((tm, D_MODEL), F32)],
        compiler_params=_params("parallel", "arbitrary"),
        name="ffn",
    )(x1, x1, nw, w_up, w_up, cw, cw, cb, cb, w_down, fw)


def _tile(n, pref):
    t = min(n, pref)
    assert n % t == 0, (n, t)
    return t


def kernel(x, attn_norm_w, w_in, mlstm_conv_w, mlstm_conv_b, mlstm_igate_b, mlstm_fgate_b, lambda_q1, lambda_k1, lambda_q2, lambda_k2, diff_norm_w, mlstm_norm_w, w_out, ffn_norm_w, w_up, ffn_conv_w, ffn_conv_b, w_down, final_norm_w):
    B, S, D = x.shape
    assert D == D_MODEL and S % ML_CHUNK == 0 and attn_norm_w.shape[0] == 1
    T = B * S
    x2 = x.reshape(T, D)

    w_main = w_in[0, :, :MAIN_COLS]
    w_gate = jnp.pad(w_in[0, :, MAIN_COLS:], ((0, 0), (0, LANES - N_GATES)))
    w_proj = jnp.concatenate([w_main, w_gate], axis=1).astype(BF16)
    q, k, v, mqk, mv, mo, g = _inproj(x2, attn_norm_w, w_proj, _tile(T, 512))

    lam_vecs = jnp.concatenate([lambda_q1, lambda_k1, lambda_q2, lambda_k2], axis=0).astype(F32)
    da = _attn(q, k, v, lam_vecs, diff_norm_w, B, S, _tile(S, 512))

    nchunks = T // ML_CHUNK
    g3 = g.reshape(nchunks, ML_CHUNK, N_GATES)
    gt3 = jnp.swapaxes(g3, 1, 2)
    gate_b = jnp.concatenate([mlstm_igate_b, mlstm_fgate_b], axis=1).astype(F32)
    ml = _mlstm(mqk, mv, mo, g3, gt3, mlstm_conv_w[0], mlstm_conv_b, gate_b, gate_b.reshape(N_GATES, 1),
                mlstm_norm_w, B, S, _tile(S, 512))

    w_o = w_out[0].astype(BF16)
    x1 = _outproj(x2, da, ml, w_o[:DA_WIDTH], w_o[DA_WIDTH:], _tile(T, 512))

    out = _ffn(x1, ffn_norm_w, w_up[0].astype(BF16), ffn_conv_w[0], ffn_conv_b, w_down[0].astype(BF16),
               final_norm_w.reshape(1, D), S, _tile(S, 512), 256)
    return out.reshape(B, S, D)
```

```python
import functools
import math

import jax
import jax.numpy as jnp
import numpy as np
from jax import lax
from jax.experimental import pallas as pl
from jax.experimental.pallas import tpu as pltpu

F32 = jnp.float32
BF16 = jnp.bfloat16

D_MODEL = 1024
DA_HEADS = 4
DA_VDIM = 128
DA_QKDIM = 64
DA_WIDTH = DA_HEADS * DA_VDIM
ML_HEADS = 4
ML_DIM = 128
ML_WIDTH = ML_HEADS * ML_DIM
ML_CHUNK = 64
ML_CONV = 4
FFN_CONV = 3
D_FF = 2816
NORM_EPS = 1e-6
LAM_INIT = 0.8 - 0.6 * math.exp(-0.3 * 0)
N_GATES = 2 * ML_HEADS
N_GROUPS = 2 * DA_HEADS
LOG2E = math.log2(math.e)
ALIBI_SLOPES_LOG2 = tuple(float(np.float32(2.0 ** (-8.0 * (h + 1) / DA_HEADS) * LOG2E)) for h in range(DA_HEADS))

LANES = 128
ATT_TILE = 512
HALO = 16
NEG_BIG = -1e30
VMEM_LIMIT = 56 * 1024 * 1024

MAIN_COLS = 2 * DA_WIDTH + DA_WIDTH + 2 * ML_WIDTH + ML_WIDTH + ML_WIDTH


def _params(*sem):
    return pltpu.CompilerParams(dimension_semantics=sem, vmem_limit_bytes=VMEM_LIMIT)


def _rms(x, w):
    return x * lax.rsqrt(jnp.mean(x * x, axis=-1, keepdims=True) + NORM_EPS) * w


def _score_aug():
    r = np.arange(ATT_TILE)
    hi = ((r // 16) * 16).astype(np.float32)
    lo = (r % 16).astype(np.float32)
    augk = np.zeros((ATT_TILE, N_GROUPS * LANES), np.float32)
    augq_t = np.zeros((N_GROUPS * LANES, ATT_TILE), np.float32)
    for g in range(N_GROUPS):
        rest = np.float32(ALIBI_SLOPES_LOG2[g // 2])
        parts = []
        for _ in range(3):
            parts.append(np.asarray(rest).astype(BF16).astype(np.float32))
            rest = np.float32(rest - parts[-1])
        assert rest == 0.0
        c0 = g * LANES + DA_QKDIM
        for i, ci in enumerate(parts):
            augk[:, c0 + i] = hi
            augk[:, c0 + 3 + i] = lo
            augk[:, c0 + 6 + i] = -ci
            augk[:, c0 + 9 + i] = -ci
            augq_t[c0 + i] = ci
            augq_t[c0 + 3 + i] = ci
            augq_t[c0 + 6 + i] = hi
            augq_t[c0 + 9 + i] = lo
    return augk, augq_t


def _inproj_kernel(x_ref, nw_ref, wr_ref, wt_ref, augk_ref, augq_ref,
                   qt_ref, ka_ref, vt_ref, mqk_ref, mv_ref, mo_ref, g_ref):
    hn = _rms(x_ref[...], nw_ref[...]).astype(BF16)

    def mm(lo, hi):
        return jnp.dot(hn, wr_ref[:, lo:hi], preferred_element_type=F32)

    def mm_t(lo, hi):
        return lax.dot_general(wt_ref[lo:hi, :], hn, (((1,), (1,)), ((), ())), preferred_element_type=F32)

    for r in range(0, N_GROUPS * LANES, 512):
        qt_ref[0, r:r + 512, :] = (mm_t(r, r + 512) * (DA_QKDIM ** -0.5 * LOG2E) + augq_ref[r:r + 512, :]).astype(BF16)
    vt_ref[0] = mm_t(N_GROUPS * LANES, N_GROUPS * LANES + DA_WIDTH).astype(BF16)

    c = 0
    for r in range(0, N_GROUPS * LANES, 512):
        ka_ref[:, r:r + 512] = (mm(c, c + 512) + augk_ref[:, r:r + 512]).astype(BF16); c += 512
    mqk_ref[:, :512] = mm(c, c + 512).astype(BF16); c += 512
    mqk_ref[:, 512:] = mm(c, c + 512).astype(BF16); c += 512
    mv_ref[...] = mm(c, c + 512).astype(BF16); c += 512
    mo_ref[...] = mm(c, c + 512).astype(BF16); c += 512
    g_ref[...] = mm(c, c + LANES)[:, :N_GATES]


def _inproj(x2, nw, w_row, w_t, augk, augq_t):
    T = x2.shape[0]
    tm = ATT_TILE
    nt = T // tm
    row = lambda i: (i, 0)
    row3 = lambda i: (i, 0, 0)
    const = lambda i: (0, 0)
    bf = lambda n: jax.ShapeDtypeStruct((T, n), BF16)
    ga = N_GROUPS * LANES
    return pl.pallas_call(
        _inproj_kernel,
        grid=(nt,),
        in_specs=[pl.BlockSpec((tm, D_MODEL), row),
                  pl.BlockSpec((1, D_MODEL), const),
                  pl.BlockSpec(w_row.shape, const),
                  pl.BlockSpec(w_t.shape, const),
                  pl.BlockSpec((tm, ga), const),
                  pl.BlockSpec((ga, tm), const)],
        out_specs=[pl.BlockSpec((1, ga, tm), row3), pl.BlockSpec((tm, ga), row),
                   pl.BlockSpec((1, DA_WIDTH, tm), row3), pl.BlockSpec((tm, 1024), row),
                   pl.BlockSpec((tm, 512), row), pl.BlockSpec((tm, 512), row),
                   pl.BlockSpec((tm, N_GATES), row)],
        out_shape=[jax.ShapeDtypeStruct((nt, ga, tm), BF16), bf(ga),
                   jax.ShapeDtypeStruct((nt, DA_WIDTH, tm), BF16), bf(1024), bf(512), bf(512),
                   jax.ShapeDtypeStruct((T, N_GATES), F32)],
        compiler_params=_params("parallel"),
        name="inproj",
    )(x2, nw, w_row, w_t, augk, augq_t)


def _attn_kernel(qt_ref, ka_ref, vt_ref, lam_ref, nw_ref, o_ref, s_sc, acc_sc):
    t = ATT_TILE
    qi = pl.program_id(1)
    lv = lam_ref[...]
    lam = (jnp.exp(jnp.sum(lv[0:1] * lv[1:2], axis=-1, keepdims=True))
           - jnp.exp(jnp.sum(lv[2:3] * lv[3:4], axis=-1, keepdims=True)) + LAM_INIT)
    keep = lax.broadcasted_iota(jnp.int32, (t, t), 0) <= lax.broadcasted_iota(jnp.int32, (t, t), 1)

    for h in range(DA_HEADS):
        slope = ALIBI_SLOPES_LOG2[h]

        def scores(ki, masked):
            k0 = pl.multiple_of(ki * t, t)
            mts = []
            for c in range(2):
                g = 2 * h + c
                s = jnp.dot(ka_ref[pl.ds(k0, t), g * LANES:(g + 1) * LANES], qt_ref[0, g * LANES:(g + 1) * LANES, :],
                            preferred_element_type=F32)
                if masked:
                    s = jnp.where(keep, s, NEG_BIG)
                s_sc[c] = s
                mts.append(jnp.max(s, axis=0, keepdims=True))
            return mts

        def accumulate(ki, mts, stats):
            vt = vt_ref[ki, h * DA_VDIM:(h + 1) * DA_VDIM, :]
            shift = (slope * t) * (ki - qi).astype(F32)
            out = []
            for c in range(2):
                m_old, l_old = stats[2 * c], stats[2 * c + 1]
                m_new = jnp.maximum(m_old, mts[c] + shift)
                alpha = jnp.exp2(m_old - m_new)
                p = jnp.exp2(s_sc[c] - (m_new - shift))
                l_new = alpha * l_old + jnp.sum(p, axis=0, keepdims=True)
                acc_sc[c] = alpha * acc_sc[c] + jnp.dot(vt, p.astype(BF16), preferred_element_type=F32)
                out += [m_new, l_new]
            return out

        def trip(ki, carry):
            prev, mt0, mt1 = carry[:3]
            stats = accumulate(prev, (mt0, mt1), carry[3:])
            mt0, mt1 = scores(ki, False)
            return (ki, mt0, mt1, *stats)

        acc_sc[...] = jnp.zeros(acc_sc.shape, F32)
        neg = jnp.full((1, t), NEG_BIG, F32)
        zero = jnp.zeros((1, t), F32)
        mt0, mt1 = scores(qi, True)
        carry = lax.fori_loop(0, qi, trip, (qi, mt0, mt1, neg, zero, neg, zero))
        _, l0, _, l1 = accumulate(carry[0], carry[1:3], carry[3:])

        o = acc_sc[0] / l0 - lam * (acc_sc[1] / l1)
        o = o * lax.rsqrt(jnp.mean(o * o, axis=0, keepdims=True) + NORM_EPS) * nw_ref[...] * (1.0 - LAM_INIT)
        o_ref[:, h * DA_VDIM:(h + 1) * DA_VDIM] = o.T.astype(BF16)


def _attn(qt, ka, vt, lam_vecs, nw_col, B, S):
    t = ATT_TILE
    nq = S // t
    ga = N_GROUPS * LANES
    return pl.pallas_call(
        _attn_kernel,
        grid=(B, nq),
        in_specs=[pl.BlockSpec((1, ga, t), lambda b, i: (b * nq + i, 0, 0)),
                  pl.BlockSpec((S, ga), lambda b, i: (b, 0)),
                  pl.BlockSpec((nq, DA_WIDTH, t), lambda b, i: (b, 0, 0)),
                  pl.BlockSpec((4, DA_QKDIM), lambda b, i: (0, 0)),
                  pl.BlockSpec((DA_VDIM, 1), lambda b, i: (0, 0))],
        out_specs=pl.BlockSpec((t, DA_WIDTH), lambda b, i: (b * nq + i, 0)),
        out_shape=jax.ShapeDtypeStruct((B * S, DA_WIDTH), BF16),
        scratch_shapes=[pltpu.VMEM((2, t, t), F32), pltpu.VMEM((2, DA_VDIM, t), F32)],
        compiler_params=_params("parallel", "arbitrary"),
        name="attn",
    )(qt, ka, vt, lam_vecs, nw_col)


def _log_sigmoid(x):
    return jnp.minimum(x, 0.0) - jnp.log(1.0 + jnp.exp(-jnp.abs(x)))


def _mlstm_kernel(qk_ref, halo_ref, v_ref, o_ref, g_ref, gt_ref, cw_ref, cb_ref, gbr_ref, gbc_ref,
                  nw_ref, out_ref, xx_sc, qk_sc, c_sc, m_sc, *, ts):
    j = pl.program_id(1)
    L = ML_CHUNK

    @pl.when(j == 0)
    def _():
        c_sc[...] = jnp.zeros(c_sc.shape, F32)
        m_sc[...] = jnp.zeros(m_sc.shape, F32)
        xx_sc[0:HALO, :] = jnp.zeros((HALO, 2 * ML_WIDTH), F32)

    @pl.when(j > 0)
    def _():
        xx_sc[0:HALO, :] = halo_ref[...].astype(F32)

    xx_sc[HALO:, :] = qk_ref[...].astype(F32)
    y = cb_ref[...]
    for t in range(ML_CONV):
        off = HALO - (ML_CONV - 1) + t
        y = y + cw_ref[t:t + 1, :] * xx_sc[off:off + ts, :]
    y = y / (1.0 + jnp.exp(-y))
    qk_sc[:, :ML_WIDTH] = y[:, :ML_WIDTH].astype(BF16)
    qk_sc[:, ML_WIDTH:] = (y[:, ML_WIDTH:] * (ML_DIM ** -0.5)).astype(BF16)

    tt = lax.broadcasted_iota(jnp.int32, (L, L), 0)
    ss = lax.broadcasted_iota(jnp.int32, (L, L), 1)
    causal = ss <= tt
    lane = lax.broadcasted_iota(jnp.int32, (L, ML_DIM), 1)
    ones_col = jnp.where(lane == 0, 1.0, 0.0).astype(F32)

    def chunk(ci, carry):
        r0 = pl.multiple_of(ci * L, L)
        g = g_ref[ci] + gbr_ref[...]
        gt = gt_ref[ci] + gbc_ref[...]
        for h in range(ML_HEADS):
            i_col = g[:, h:h + 1]
            f_col = _log_sigmoid(g[:, ML_HEADS + h:ML_HEADS + h + 1])
            i_row = gt[h:h + 1, :]
            f_row = _log_sigmoid(gt[ML_HEADS + h:ML_HEADS + h + 1, :])
            b_col = jnp.sum(jnp.where(causal, f_row, 0.0), axis=1, keepdims=True)
            b_row = jnp.sum(jnp.where(tt <= ss, f_col, 0.0), axis=0, keepdims=True)
            log_d = jnp.where(causal, b_col - b_row + i_row, NEG_BIG)
            m_old = m_sc[h]
            log_prev = b_col + m_old
            m_t = jnp.maximum(log_prev, jnp.max(log_d, axis=1, keepdims=True))
            d_mat = jnp.exp(log_d - m_t)
            w_prev = jnp.exp(log_prev - m_t)

            hs = slice(h * ML_DIM, (h + 1) * ML_DIM)
            qh = qk_sc[pl.ds(r0, L), h * ML_DIM:(h + 1) * ML_DIM]
            kh = qk_sc[pl.ds(r0, L), ML_WIDTH + h * ML_DIM:ML_WIDTH + (h + 1) * ML_DIM]
            vh = v_ref[pl.ds(r0, L), hs].astype(F32)
            v_aug = jnp.concatenate([vh, ones_col], axis=1)

            sc = lax.dot_general(qh, kh, (((1,), (1,)), ((), ())), preferred_element_type=F32) * d_mat
            c_old = c_sc[h]
            inter = jnp.dot(qh, c_old.astype(BF16), preferred_element_type=F32)
            intra = jnp.dot(sc.astype(BF16), v_aug.astype(BF16), preferred_element_type=F32)
            tot = intra + w_prev * inter
            num = tot[:, :ML_DIM]
            den = tot[:, ML_DIM:ML_DIM + 1]
            hh = num / jnp.maximum(jnp.abs(den), jnp.exp(-m_t))

            b_last = b_col[L - 1:L, :]
            m_last = m_t[L - 1:L, :]
            w_state = jnp.exp(b_last - b_col + i_col - m_last)
            w_last = w_prev[L - 1:L, :]
            upd = lax.dot_general(kh, (w_state * v_aug).astype(BF16), (((0,), (0,)), ((), ())),
                                  preferred_element_type=F32)
            c_sc[h] = w_last * c_old + upd
            m_sc[h] = m_last

            hh = _rms(hh, nw_ref[:, hs])
            gate = o_ref[pl.ds(r0, L), hs].astype(F32)
            out_ref[pl.ds(r0, L), hs] = (hh / (1.0 + jnp.exp(-gate))).astype(BF16)
        return carry

    lax.fori_loop(0, ts // L, chunk, 0)


def _mlstm(mqk, mv, mo, g3, gt3, cw, cb, gb_row, gb_col, nw, B, S, ts):
    ns = S // ts
    cpb = ts // ML_CHUNK
    row = lambda b, j: (b * ns + j, 0)
    halo = lambda b, j: (jnp.maximum((b * S + j * ts) // HALO - 1, 0), 0)
    ch = lambda b, j: (b * ns + j, 0, 0)
    const = lambda b, j: (0, 0)
    return pl.pallas_call(
        functools.partial(_mlstm_kernel, ts=ts),
        grid=(B, ns),
        in_specs=[pl.BlockSpec((ts, 2 * ML_WIDTH), row),
                  pl.BlockSpec((HALO, 2 * ML_WIDTH), halo),
                  pl.BlockSpec((ts, ML_WIDTH), row),
                  pl.BlockSpec((ts, ML_WIDTH), row),
                  pl.BlockSpec((cpb, ML_CHUNK, N_GATES), ch),
                  pl.BlockSpec((cpb, N_GATES, ML_CHUNK), ch),
                  pl.BlockSpec((ML_CONV, 2 * ML_WIDTH), const),
                  pl.BlockSpec((1, 2 * ML_WIDTH), const),
                  pl.BlockSpec((1, N_GATES), const),
                  pl.BlockSpec((N_GATES, 1), const),
                  pl.BlockSpec((1, ML_WIDTH), const)],
        out_specs=pl.BlockSpec((ts, ML_WIDTH), row),
        out_shape=jax.ShapeDtypeStruct((B * S, ML_WIDTH), BF16),
        scratch_shapes=[pltpu.VMEM((ts + HALO, 2 * ML_WIDTH), F32),
                        pltpu.VMEM((ts, 2 * ML_WIDTH), BF16),
                        pltpu.VMEM((ML_HEADS, ML_DIM, 2 * ML_DIM), F32),
                        pltpu.VMEM((ML_HEADS, 1, 1), F32)],
        compiler_params=_params("parallel", "arbitrary"),
        name="mlstm",
    )(mqk, mqk, mv, mo, g3, gt3, cw, cb, gb_row, gb_col, nw)


def _outproj_kernel(x_ref, da_ref, ml_ref, wa_ref, wm_ref, o_ref):
    o_ref[...] = (x_ref[...]
                  + jnp.dot(da_ref[...], wa_ref[...], preferred_element_type=F32)
                  + jnp.dot(ml_ref[...], wm_ref[...], preferred_element_type=F32))


def _outproj(x2, da, ml, wa, wm, tm):
    T = x2.shape[0]
    row = lambda i: (i, 0)
    const = lambda i: (0, 0)
    return pl.pallas_call(
        _outproj_kernel,
        grid=(T // tm,),
        in_specs=[pl.BlockSpec((tm, D_MODEL), row),
                  pl.BlockSpec((tm, DA_WIDTH), row),
                  pl.BlockSpec((tm, ML_WIDTH), row),
                  pl.BlockSpec((DA_WIDTH, D_MODEL), const),
                  pl.BlockSpec((ML_WIDTH, D_MODEL), const)],
        out_specs=pl.BlockSpec((tm, D_MODEL), row),
        out_shape=jax.ShapeDtypeStruct((T, D_MODEL), F32),
        compiler_params=_params("parallel"),
        name="outproj",
    )(x2, da, ml, wa, wm)


def _ffn_kernel(x_ref, halo_ref, nw_ref, wg_ref, wu_ref, cwg_ref, cwu_ref, cbg_ref, cbu_ref, wd_ref,
                fw_ref, o_ref, hn_sc, ug_sc, uu_sc, acc_sc, *, tm, tiles_per_seq):
    i = pl.program_id(0)
    f = pl.program_id(1)

    @pl.when(f == 0)
    def _():
        hn_sc[HALO:, :] = _rms(x_ref[...], nw_ref[...]).astype(BF16)
        acc_sc[...] = jnp.zeros(acc_sc.shape, F32)

    @pl.when((f == 0) & (i % tiles_per_seq == 0))
    def _():
        hn_sc[0:HALO, :] = jnp.zeros((HALO, D_MODEL), BF16)

    @pl.when((f == 0) & (i % tiles_per_seq != 0))
    def _():
        hn_sc[0:HALO, :] = _rms(halo_ref[...], nw_ref[...]).astype(BF16)

    hn = hn_sc[...]
    ug_sc[...] = jnp.dot(hn, wg_ref[...], preferred_element_type=F32)
    uu_sc[...] = jnp.dot(hn, wu_ref[...], preferred_element_type=F32)
    yg = cbg_ref[...]
    yu = cbu_ref[...]
    for t in range(FFN_CONV):
        off = HALO - (FFN_CONV - 1) + t
        yg = yg + cwg_ref[t:t + 1, :] * ug_sc[off:off + tm, :]
        yu = yu + cwu_ref[t:t + 1, :] * uu_sc[off:off + tm, :]
    act = (yg / (1.0 + jnp.exp(-yg))) * yu
    acc_sc[...] += jnp.dot(act.astype(BF16), wd_ref[...], preferred_element_type=F32)

    @pl.when(f == pl.num_programs(1) - 1)
    def _():
        o_ref[...] = _rms(x_ref[...] + acc_sc[...], fw_ref[...])


def _ffn(x1, nw, w_up, cw, cb, w_down, fw, S, tm, fc):
    T = x1.shape[0]
    nf = D_FF // fc
    row = lambda i, f: (i, 0)
    halo = lambda i, f: (jnp.maximum(i * (tm // HALO) - 1, 0), 0)
    const = lambda i, f: (0, 0)
    gcol = lambda i, f: (0, f)
    ucol = lambda i, f: (0, nf + f)
    return pl.pallas_call(
        functools.partial(_ffn_kernel, tm=tm, tiles_per_seq=S // tm),
        grid=(T // tm, nf),
        in_specs=[pl.BlockSpec((tm, D_MODEL), row),
                  pl.BlockSpec((HALO, D_MODEL), halo),
                  pl.BlockSpec((1, D_MODEL), const),
                  pl.BlockSpec((D_MODEL, fc), gcol),
                  pl.BlockSpec((D_MODEL, fc), ucol),
                  pl.BlockSpec((FFN_CONV, fc), gcol),
                  pl.BlockSpec((FFN_CONV, fc), ucol),
                  pl.BlockSpec((1, fc), gcol),
                  pl.BlockSpec((1, fc), ucol),
                  pl.BlockSpec((fc, D_MODEL), lambda i, f: (f, 0)),
                  pl.BlockSpec((1, D_MODEL), const)],
        out_specs=pl.BlockSpec((tm, D_MODEL), row),
        out_shape=jax.ShapeDtypeStruct((T, D_MODEL), F32),
        scratch_shapes=[pltpu.VMEM((tm + HALO, D_MODEL), BF16),
                        pltpu.VMEM((tm + HALO, fc), F32),
                        pltpu.VMEM((tm + HALO, fc), F32),
                        pltpu.VMEM((tm, D_MODEL), F32)],
        compiler_params=_params("parallel", "arbitrary"),
        name="ffn",
    )(x1, x1, nw, w_up, w_up, cw, cw, cb, cb, w_down, fw)


def _tile(n, pref):
    t = min(n, pref)
    assert n % t == 0, (n, t)
    return t


def kernel(x, attn_norm_w, w_in, mlstm_conv_w, mlstm_conv_b, mlstm_igate_b, mlstm_fgate_b, lambda_q1, lambda_k1, lambda_q2, lambda_k2, diff_norm_w, mlstm_norm_w, w_out, ffn_norm_w, w_up, ffn_conv_w, ffn_conv_b, w_down, final_norm_w):
    B, S, D = x.shape
    assert D == D_MODEL and S % ATT_TILE == 0 and attn_norm_w.shape[0] == 1
    T = B * S
    x2 = x.reshape(T, D)

    w = w_in[0]

    def spread(wc):
        return jnp.pad(wc.reshape(D, N_GROUPS, DA_QKDIM), ((0, 0), (0, 0), (0, LANES - DA_QKDIM))).reshape(D, -1)

    w_gate = jnp.pad(w[:, MAIN_COLS:], ((0, 0), (0, LANES - N_GATES)))
    w_row = jnp.concatenate([spread(w[:, 512:1024]), w[:, 1536:MAIN_COLS], w_gate], axis=1).astype(BF16)
    w_t = jnp.concatenate([spread(w[:, :512]), w[:, 1024:1536]], axis=1).T.astype(BF16)
    augk, augq_t = _score_aug()
    qt, ka, vt, mqk, mv, mo, g = _inproj(x2, attn_norm_w, w_row, w_t, jnp.asarray(augk), jnp.asarray(augq_t))

    lam_vecs = jnp.concatenate([lambda_q1, lambda_k1, lambda_q2, lambda_k2], axis=0).astype(F32)
    da = _attn(qt, ka, vt, lam_vecs, diff_norm_w.reshape(DA_VDIM, 1), B, S)

    nchunks = T // ML_CHUNK
    g3 = g.reshape(nchunks, ML_CHUNK, N_GATES)
    gt3 = jnp.swapaxes(g3, 1, 2)
    gate_b = jnp.concatenate([mlstm_igate_b, mlstm_fgate_b], axis=1).astype(F32)
    ml = _mlstm(mqk, mv, mo, g3, gt3, mlstm_conv_w[0], mlstm_conv_b, gate_b, gate_b.reshape(N_GATES, 1),
                mlstm_norm_w, B, S, _tile(S, 512))

    w_o = w_out[0].astype(BF16)
    x1 = _outproj(x2, da, ml, w_o[:DA_WIDTH], w_o[DA_WIDTH:], _tile(T, 512))

    out = _ffn(x1, ffn_norm_w, w_up[0].astype(BF16), ffn_conv_w[0], ffn_conv_b, w_down[0].astype(BF16),
               final_norm_w.reshape(1, D), S, _tile(S, 512), 256)
    return out.reshape(B, S, D)
```

```python
import functools
import math

import jax
import jax.numpy as jnp
import numpy as np
from jax import lax
from jax.experimental import pallas as pl
from jax.experimental.pallas import tpu as pltpu

F32 = jnp.float32
BF16 = jnp.bfloat16

D_MODEL = 1024
DA_HEADS = 4
DA_VDIM = 128
DA_QKDIM = 64
DA_WIDTH = DA_HEADS * DA_VDIM
ML_HEADS = 4
ML_DIM = 128
ML_WIDTH = ML_HEADS * ML_DIM
ML_CHUNK = 64
ML_CONV = 4
FFN_CONV = 3
D_FF = 2816
NORM_EPS = 1e-6
LAM_INIT = 0.8 - 0.6 * math.exp(-0.3 * 0)
N_GATES = 2 * ML_HEADS
N_GROUPS = 2 * DA_HEADS
LOG2E = math.log2(math.e)
ALIBI_SLOPES_LOG2 = tuple(float(np.float32(2.0 ** (-8.0 * (h + 1) / DA_HEADS) * LOG2E)) for h in range(DA_HEADS))

LANES = 128
ATT_TILE = 512
HALO = 16
NEG_BIG = -1e30
VMEM_LIMIT = 56 * 1024 * 1024

MAIN_COLS = 2 * DA_WIDTH + DA_WIDTH + 2 * ML_WIDTH + ML_WIDTH + ML_WIDTH


def _params(*sem):
    return pltpu.CompilerParams(dimension_semantics=sem, vmem_limit_bytes=VMEM_LIMIT)


def _rms(x, w):
    return x * lax.rsqrt(jnp.mean(x * x, axis=-1, keepdims=True) + NORM_EPS) * w


def _score_aug():
    r = np.arange(ATT_TILE)
    hi = ((r // 16) * 16).astype(np.float32)
    lo = (r % 16).astype(np.float32)
    augk = np.zeros((ATT_TILE, N_GROUPS * LANES), np.float32)
    augq_t = np.zeros((N_GROUPS * LANES, ATT_TILE), np.float32)
    for g in range(N_GROUPS):
        rest = np.float32(ALIBI_SLOPES_LOG2[g // 2])
        parts = []
        for _ in range(3):
            parts.append(np.asarray(rest).astype(BF16).astype(np.float32))
            rest = np.float32(rest - parts[-1])
        assert rest == 0.0
        c0 = g * LANES + (DA_QKDIM if g % 2 == 0 else 0)
        for i, ci in enumerate(parts):
            augk[:, c0 + i] = hi
            augk[:, c0 + 3 + i] = lo
            augk[:, c0 + 6 + i] = -ci
            augk[:, c0 + 9 + i] = -ci
            augq_t[c0 + i] = ci
            augq_t[c0 + 3 + i] = ci
            augq_t[c0 + 6 + i] = hi
            augq_t[c0 + 9 + i] = lo
    return augk, augq_t


def _inproj_kernel(x_ref, xh_ref, nw_ref, wr_ref, wt_ref, augk_ref, augq_ref, cw_ref, cb_ref,
                   qt_ref, ka_ref, vt_ref, mqk_ref, mv_ref, mo_ref, g_ref, pre_sc, *, tiles_per_seq):
    i = pl.program_id(0)
    tm = ATT_TILE
    hn = _rms(x_ref[...], nw_ref[...]).astype(BF16)

    def mm(lo, hi):
        return jnp.dot(hn, wr_ref[:, lo:hi], preferred_element_type=F32)

    def mm_t(lo, hi):
        return lax.dot_general(wt_ref[lo:hi, :], hn, (((1,), (1,)), ((), ())), preferred_element_type=F32)

    qq = mm_t(0, DA_WIDTH) * (DA_QKDIM ** -0.5 * LOG2E)
    kk = mm(0, DA_WIDTH)
    vt_ref[0] = mm_t(DA_WIDTH, 2 * DA_WIDTH).astype(BF16)
    lane = lax.broadcasted_iota(jnp.int32, (tm, LANES), 1)
    for g in range(N_GROUPS):
        h, c = divmod(g, 2)
        src = h * LANES + c * DA_QKDIM
        dst = g * LANES + c * DA_QKDIM
        spare = g * LANES + (1 - c) * DA_QKDIM
        qt_ref[0, dst:dst + DA_QKDIM, :] = qq[src:src + DA_QKDIM, :].astype(BF16)
        qt_ref[0, spare:spare + DA_QKDIM, :] = augq_ref[spare:spare + DA_QKDIM, :].astype(BF16)
        is_data = (lane < DA_QKDIM) if c == 0 else (lane >= DA_QKDIM)
        ka_ref[:, g * LANES:(g + 1) * LANES] = jnp.where(
            is_data, kk[:, (g // 2) * LANES:(g // 2 + 1) * LANES], augk_ref[:, g * LANES:(g + 1) * LANES]).astype(BF16)

    c = DA_WIDTH
    @pl.when(i % tiles_per_seq == 0)
    def _():
        pre_sc[0:HALO, :] = jnp.zeros((HALO, 2 * ML_WIDTH), F32)

    @pl.when(i % tiles_per_seq != 0)
    def _():
        hh = _rms(xh_ref[...], nw_ref[...]).astype(BF16)
        pre_sc[0:HALO, :] = jnp.dot(hh, wr_ref[:, c:c + 2 * ML_WIDTH], preferred_element_type=F32)

    pre_sc[HALO:, :512] = mm(c, c + 512); c += 512
    pre_sc[HALO:, 512:] = mm(c, c + 512); c += 512
    y = cb_ref[...]
    for t in range(ML_CONV):
        off = HALO - (ML_CONV - 1) + t
        y = y + cw_ref[t:t + 1, :] * pre_sc[off:off + tm, :]
    y = y / (1.0 + jnp.exp(-y))
    mqk_ref[:, :ML_WIDTH] = y[:, :ML_WIDTH].astype(BF16)
    mqk_ref[:, ML_WIDTH:] = (y[:, ML_WIDTH:] * (ML_DIM ** -0.5)).astype(BF16)
    mv_ref[...] = mm(c, c + 512).astype(BF16); c += 512
    mo_ref[...] = mm(c, c + 512).astype(BF16); c += 512
    g_ref[...] = mm(c, c + LANES)[:, :N_GATES]


def _inproj(x2, nw, w_row, w_t, augk, augq_t, cw, cb, S):
    T = x2.shape[0]
    tm = ATT_TILE
    nt = T // tm
    row = lambda i: (i, 0)
    row3 = lambda i: (i, 0, 0)
    halo = lambda i: (jnp.maximum(i * (tm // HALO) - 1, 0), 0)
    const = lambda i: (0, 0)
    bf = lambda n: jax.ShapeDtypeStruct((T, n), BF16)
    ga = N_GROUPS * LANES
    return pl.pallas_call(
        functools.partial(_inproj_kernel, tiles_per_seq=S // tm),
        grid=(nt,),
        in_specs=[pl.BlockSpec((tm, D_MODEL), row),
                  pl.BlockSpec((HALO, D_MODEL), halo),
                  pl.BlockSpec((1, D_MODEL), const),
                  pl.BlockSpec(w_row.shape, const),
                  pl.BlockSpec(w_t.shape, const),
                  pl.BlockSpec((tm, ga), const),
                  pl.BlockSpec((ga, tm), const),
                  pl.BlockSpec((ML_CONV, 2 * ML_WIDTH), const),
                  pl.BlockSpec((1, 2 * ML_WIDTH), const)],
        out_specs=[pl.BlockSpec((1, ga, tm), row3), pl.BlockSpec((tm, ga), row),
                   pl.BlockSpec((1, DA_WIDTH, tm), row3), pl.BlockSpec((tm, 1024), row),
                   pl.BlockSpec((tm, 512), row), pl.BlockSpec((tm, 512), row),
                   pl.BlockSpec((tm, N_GATES), row)],
        out_shape=[jax.ShapeDtypeStruct((nt, ga, tm), BF16), bf(ga),
                   jax.ShapeDtypeStruct((nt, DA_WIDTH, tm), BF16), bf(1024), bf(512), bf(512),
                   jax.ShapeDtypeStruct((T, N_GATES), F32)],
        scratch_shapes=[pltpu.VMEM((tm + HALO, 2 * ML_WIDTH), F32)],
        compiler_params=_params("parallel"),
        name="inproj",
    )(x2, x2, nw, w_row, w_t, augk, augq_t, cw, cb)


def _attn_kernel(qt_ref, ka_ref, vt_ref, lam_ref, nw_ref, o_ref, s_sc, acc_sc):
    t = ATT_TILE
    qi = pl.program_id(1)
    lv = lam_ref[...]
    lam = (jnp.exp(jnp.sum(lv[0:1] * lv[1:2], axis=-1, keepdims=True))
           - jnp.exp(jnp.sum(lv[2:3] * lv[3:4], axis=-1, keepdims=True)) + LAM_INIT)
    keep = lax.broadcasted_iota(jnp.int32, (t, t), 0) <= lax.broadcasted_iota(jnp.int32, (t, t), 1)

    def scores(h, ki, masked):
        k0 = pl.multiple_of(ki * t, t)
        mts = []
        for c in range(2):
            g = 2 * h + c
            s = jnp.dot(ka_ref[pl.ds(k0, t), g * LANES:(g + 1) * LANES], qt_ref[0, g * LANES:(g + 1) * LANES, :],
                        preferred_element_type=F32)
            if masked:
                s = jnp.where(keep, s, NEG_BIG)
            s_sc[g] = s
            mts.append(jnp.max(s, axis=0, keepdims=True))
        return mts

    def accumulate(h, ki, mts, stats):
        vt = vt_ref[ki, h * DA_VDIM:(h + 1) * DA_VDIM, :]
        shift = (ALIBI_SLOPES_LOG2[h] * t) * (ki - qi).astype(F32)
        out = []
        for c in range(2):
            g = 2 * h + c
            m_old, l_old = stats[2 * c], stats[2 * c + 1]
            m_new = jnp.maximum(m_old, mts[c] + shift)
            alpha = jnp.exp2(m_old - m_new)
            p = jnp.exp2(s_sc[g] - (m_new - shift))
            l_new = alpha * l_old + jnp.sum(p, axis=0, keepdims=True)
            acc_sc[g] = alpha * acc_sc[g] + jnp.dot(vt, p.astype(BF16), preferred_element_type=F32)
            out += [m_new, l_new]
        return out

    def trip(ki, carry):
        prev = carry[0]
        out = [ki]
        for h in range(DA_HEADS):
            ch = carry[1 + 6 * h:7 + 6 * h]
            stats = accumulate(h, prev, ch[:2], ch[2:])
            out += scores(h, ki, False) + stats
        return tuple(out)

    acc_sc[...] = jnp.zeros(acc_sc.shape, F32)
    neg = jnp.full((1, t), NEG_BIG, F32)
    zero = jnp.zeros((1, t), F32)
    init = [qi]
    for h in range(DA_HEADS):
        init += scores(h, qi, True) + [neg, zero, neg, zero]
    carry = lax.fori_loop(0, qi, trip, tuple(init))
    for h in range(DA_HEADS):
        ch = carry[1 + 6 * h:7 + 6 * h]
        _, l0, _, l1 = accumulate(h, carry[0], ch[:2], ch[2:])
        o = acc_sc[2 * h] / l0 - lam * (acc_sc[2 * h + 1] / l1)
        o = o * lax.rsqrt(jnp.mean(o * o, axis=0, keepdims=True) + NORM_EPS) * nw_ref[...] * (1.0 - LAM_INIT)
        o_ref[:, h * DA_VDIM:(h + 1) * DA_VDIM] = o.T.astype(BF16)


def _attn(qt, ka, vt, lam_vecs, nw_col, B, S):
    t = ATT_TILE
    nq = S // t
    ga = N_GROUPS * LANES
    return pl.pallas_call(
        _attn_kernel,
        grid=(B, nq),
        in_specs=[pl.BlockSpec((1, ga, t), lambda b, i: (b * nq + i, 0, 0)),
                  pl.BlockSpec((S, ga), lambda b, i: (b, 0)),
                  pl.BlockSpec((nq, DA_WIDTH, t), lambda b, i: (b, 0, 0)),
                  pl.BlockSpec((4, DA_QKDIM), lambda b, i: (0, 0)),
                  pl.BlockSpec((DA_VDIM, 1), lambda b, i: (0, 0))],
        out_specs=pl.BlockSpec((t, DA_WIDTH), lambda b, i: (b * nq + i, 0)),
        out_shape=jax.ShapeDtypeStruct((B * S, DA_WIDTH), BF16),
        scratch_shapes=[pltpu.VMEM((N_GROUPS, t, t), F32), pltpu.VMEM((N_GROUPS, DA_VDIM, t), F32)],
        compiler_params=_params("parallel", "arbitrary"),
        name="attn",
    )(qt, ka, vt, lam_vecs, nw_col)


def _log_sigmoid(x):
    return jnp.minimum(x, 0.0) - jnp.log(1.0 + jnp.exp(-jnp.abs(x)))


def _mlstm_kernel(qk_sc, v_ref, o_ref, g_ref, gt_ref, gbr_ref, gbc_ref,
                  nw_ref, out_ref, c_sc, m_sc, cprev_sc, mprev_sc, *, ts):
    j = pl.program_id(1)
    L = ML_CHUNK

    @pl.when(j == 0)
    def _():
        c_sc[...] = jnp.zeros(c_sc.shape, F32)
        m_sc[...] = jnp.zeros(m_sc.shape, F32)

    tt = lax.broadcasted_iota(jnp.int32, (L, L), 0)
    ss = lax.broadcasted_iota(jnp.int32, (L, L), 1)
    causal = ss <= tt
    nc = ts // L
    lane = lax.broadcasted_iota(jnp.int32, (nc, L, ML_DIM), 2)
    ones_col = jnp.where(lane == 0, 1.0, 0.0).astype(BF16)
    g = g_ref[...] + gbr_ref[...]
    gt = gt_ref[...] + gbc_ref[...]
    lf, lft = _log_sigmoid(g), _log_sigmoid(gt)
    bdot = lambda x, y, cx, cy: lax.dot_general(x, y, (((cx,), (cy,)), ((0,), (0,))), preferred_element_type=F32)

    for h in range(ML_HEADS):
        hs = slice(h * ML_DIM, (h + 1) * ML_DIM)
        i_col, f_col = g[:, :, h:h + 1], lf[:, :, ML_HEADS + h:ML_HEADS + h + 1]
        i_row, f_row = gt[:, h:h + 1, :], lft[:, ML_HEADS + h:ML_HEADS + h + 1, :]
        b_col = jnp.sum(jnp.where(causal, f_row, 0.0), axis=2, keepdims=True)
        b_row = jnp.sum(jnp.where(tt <= ss, f_col, 0.0), axis=1, keepdims=True)
        log_d = jnp.where(causal, b_col - b_row + i_row, NEG_BIG)
        a = jnp.max(log_d, axis=2, keepdims=True)
        d_loc = jnp.exp(log_d - a)

        q3 = qk_sc[:, hs].reshape(nc, L, ML_DIM)
        k3 = qk_sc[:, ML_WIDTH + h * ML_DIM:ML_WIDTH + (h + 1) * ML_DIM].reshape(nc, L, ML_DIM)
        v_aug = jnp.concatenate([v_ref[:, hs].reshape(nc, L, ML_DIM), ones_col], axis=2)

        sc = bdot(q3, k3, 2, 2) * d_loc
        intra = bdot(sc.astype(BF16), v_aug, 2, 1)
        b_last = b_col[:, L - 1:L, :]
        a_last = a[:, L - 1:L, :]
        w_loc = jnp.exp(b_last - b_col + i_col - a_last)
        wv = (w_loc * v_aug.astype(F32)).astype(BF16)
        upd = [lax.dot_general(k3[ci], wv[ci], (((0,), (0,)), ((), ())), preferred_element_type=F32)
               for ci in range(nc)]

        c_old = c_sc[h]
        m_old = m_sc[h]
        for ci in range(nc):
            cprev_sc[ci] = c_old.astype(BF16)
            mprev_sc[ci] = m_old
            m_new = jnp.maximum(b_last[ci] + m_old, a_last[ci])
            c_old = jnp.exp(b_last[ci] + m_old - m_new) * c_old + jnp.exp(a_last[ci] - m_new) * upd[ci]
            m_old = m_new
        c_sc[h] = c_old
        m_sc[h] = m_old

        log_prev = b_col + mprev_sc[...]
        m_t = jnp.maximum(log_prev, a)
        inter = bdot(q3, cprev_sc[...], 2, 1)
        tot = jnp.exp(a - m_t) * intra + jnp.exp(log_prev - m_t) * inter
        hh = tot[:, :, :ML_DIM] / jnp.maximum(jnp.abs(tot[:, :, ML_DIM:ML_DIM + 1]), jnp.exp(-m_t))
        hh = _rms(hh, nw_ref[:, hs]).reshape(ts, ML_DIM)
        gate = o_ref[:, hs].astype(F32)
        out_ref[:, hs] = (hh / (1.0 + jnp.exp(-gate))).astype(BF16)


def _mlstm(mqk, mv, mo, g3, gt3, gb_row, gb_col, nw, B, S, ts):
    ns = S // ts
    cpb = ts // ML_CHUNK
    row = lambda b, j: (b * ns + j, 0)
    ch = lambda b, j: (b * ns + j, 0, 0)
    const = lambda b, j: (0, 0)
    return pl.pallas_call(
        functools.partial(_mlstm_kernel, ts=ts),
        grid=(B, ns),
        in_specs=[pl.BlockSpec((ts, 2 * ML_WIDTH), row),
                  pl.BlockSpec((ts, ML_WIDTH), row),
                  pl.BlockSpec((ts, ML_WIDTH), row),
                  pl.BlockSpec((cpb, ML_CHUNK, N_GATES), ch),
                  pl.BlockSpec((cpb, N_GATES, ML_CHUNK), ch),
                  pl.BlockSpec((1, N_GATES), const),
                  pl.BlockSpec((N_GATES, 1), const),
                  pl.BlockSpec((1, ML_WIDTH), const)],
        out_specs=pl.BlockSpec((ts, ML_WIDTH), row),
        out_shape=jax.ShapeDtypeStruct((B * S, ML_WIDTH), BF16),
        scratch_shapes=[pltpu.VMEM((ML_HEADS, ML_DIM, 2 * ML_DIM), F32),
                        pltpu.VMEM((ML_HEADS, 1, 1), F32),
                        pltpu.VMEM((cpb, ML_DIM, 2 * ML_DIM), BF16),
                        pltpu.VMEM((cpb, 1, 1), F32)],
        compiler_params=_params("parallel", "arbitrary"),
        name="mlstm",
    )(mqk, mv, mo, g3, gt3, gb_row, gb_col, nw)


FF_CHUNK = 256


def _mixer_kernel(x_ref, xh_ref, da_ref, dah_ref, ml_ref, mlh_ref, wa_ref, wm_ref, nw_ref, wup_ref, cw_ref,
                  cb_ref, wd_ref, fw_ref, o_ref, x1_sc, hn_sc, u_sc, act_sc, *, tm, tiles_per_seq):
    i = pl.program_id(0)

    def residual(xr, dr, mr):
        return (xr[...] + jnp.dot(dr[...], wa_ref[...], preferred_element_type=F32)
                + jnp.dot(mr[...], wm_ref[...], preferred_element_type=F32))

    x1 = residual(x_ref, da_ref, ml_ref)
    x1_sc[...] = x1
    hn_sc[HALO:, :] = _rms(x1, nw_ref[...]).astype(BF16)

    @pl.when(i % tiles_per_seq == 0)
    def _():
        hn_sc[0:HALO, :] = jnp.zeros((HALO, D_MODEL), BF16)

    @pl.when(i % tiles_per_seq != 0)
    def _():
        hn_sc[0:HALO, :] = _rms(residual(xh_ref, dah_ref, mlh_ref), nw_ref[...]).astype(BF16)

    nf = D_FF // FF_CHUNK

    def up(j):
        hn = hn_sc[...]
        for part in range(2):
            lo = part * D_FF + j * FF_CHUNK
            u_sc[j % 2, part] = jnp.dot(hn, wup_ref[:, lo:lo + FF_CHUNK], preferred_element_type=F32)

    def activate(j):
        y = []
        for part in range(2):
            lo = part * D_FF + j * FF_CHUNK
            acc = cb_ref[:, lo:lo + FF_CHUNK]
            for t in range(FFN_CONV):
                off = HALO - (FFN_CONV - 1) + t
                acc = acc + cw_ref[t:t + 1, lo:lo + FF_CHUNK] * u_sc[j % 2, part, off:off + tm, :]
            y.append(acc)
        act = (y[0] / (1.0 + jnp.exp(-y[0]))) * y[1]
        act_sc[:, j * FF_CHUNK:(j + 1) * FF_CHUNK] = act.astype(BF16)

    up(0)
    for j in range(nf):
        if j + 1 < nf:
            up(j + 1)
        activate(j)

    y = x1_sc[...] + jnp.dot(act_sc[...], wd_ref[...], preferred_element_type=F32)
    o_ref[...] = _rms(y, fw_ref[...])


def _mixer(x2, da, ml, wa, wm, nw, w_up, cw, cb, w_down, fw, S, tm):
    T = x2.shape[0]
    row = lambda i: (i, 0)
    halo = lambda i: (jnp.maximum(i * (tm // HALO) - 1, 0), 0)
    const = lambda i: (0, 0)
    resident = lambda shape: pl.BlockSpec(shape, const, pipeline_mode=pl.Buffered(1))
    return pl.pallas_call(
        functools.partial(_mixer_kernel, tm=tm, tiles_per_seq=S // tm),
        grid=(T // tm,),
        in_specs=[pl.BlockSpec((tm, D_MODEL), row), pl.BlockSpec((HALO, D_MODEL), halo),
                  pl.BlockSpec((tm, DA_WIDTH), row), pl.BlockSpec((HALO, DA_WIDTH), halo),
                  pl.BlockSpec((tm, ML_WIDTH), row), pl.BlockSpec((HALO, ML_WIDTH), halo),
                  resident((DA_WIDTH, D_MODEL)), resident((ML_WIDTH, D_MODEL)),
                  resident((1, D_MODEL)),
                  resident((D_MODEL, 2 * D_FF)),
                  resident((FFN_CONV, 2 * D_FF)),
                  resident((1, 2 * D_FF)),
                  resident((D_FF, D_MODEL)),
                  resident((1, D_MODEL))],
        out_specs=pl.BlockSpec((tm, D_MODEL), row),
        out_shape=jax.ShapeDtypeStruct((T, D_MODEL), F32),
        scratch_shapes=[pltpu.VMEM((tm, D_MODEL), F32),
                        pltpu.VMEM((tm + HALO, D_MODEL), BF16),
                        pltpu.VMEM((2, 2, tm + HALO, FF_CHUNK), F32),
                        pltpu.VMEM((tm, D_FF), BF16)],
        compiler_params=_params("parallel"),
        name="mixer",
    )(x2, x2, da, da, ml, ml, wa, wm, nw, w_up, cw, cb, w_down, fw)


def _tile(n, pref):
    t = min(n, pref)
    assert n % t == 0, (n, t)
    return t


def kernel(x, attn_norm_w, w_in, mlstm_conv_w, mlstm_conv_b, mlstm_igate_b, mlstm_fgate_b, lambda_q1, lambda_k1, lambda_q2, lambda_k2, diff_norm_w, mlstm_norm_w, w_out, ffn_norm_w, w_up, ffn_conv_w, ffn_conv_b, w_down, final_norm_w):
    B, S, D = x.shape
    assert D == D_MODEL and S % ATT_TILE == 0 and attn_norm_w.shape[0] == 1
    T = B * S
    x2 = x.reshape(T, D)

    w = w_in[0]

    w_gate = jnp.pad(w[:, MAIN_COLS:], ((0, 0), (0, LANES - N_GATES)))
    w_row = jnp.concatenate([w[:, 512:1024], w[:, 1536:MAIN_COLS], w_gate], axis=1).astype(BF16)
    w_t = jnp.concatenate([w[:, :512], w[:, 1024:1536]], axis=1).T.astype(BF16)
    augk, augq_t = _score_aug()
    qt, ka, vt, mqk, mv, mo, g = _inproj(x2, attn_norm_w, w_row, w_t, jnp.asarray(augk), jnp.asarray(augq_t),
                                         mlstm_conv_w[0], mlstm_conv_b, S)

    lam_vecs = jnp.concatenate([lambda_q1, lambda_k1, lambda_q2, lambda_k2], axis=0).astype(F32)
    da = _attn(qt, ka, vt, lam_vecs, diff_norm_w.reshape(DA_VDIM, 1), B, S)

    nchunks = T // ML_CHUNK
    g3 = g.reshape(nchunks, ML_CHUNK, N_GATES)
    gt3 = jnp.swapaxes(g3, 1, 2)
    gate_b = jnp.concatenate([mlstm_igate_b, mlstm_fgate_b], axis=1).astype(F32)
    ml = _mlstm(mqk, mv, mo, g3, gt3, gate_b, gate_b.reshape(N_GATES, 1), mlstm_norm_w, B, S, _tile(S, 512))

    w_o = w_out[0].astype(BF16)
    out = _mixer(x2, da, ml, w_o[:DA_WIDTH], w_o[DA_WIDTH:], ffn_norm_w, w_up[0].astype(BF16), ffn_conv_w[0],
                 ffn_conv_b, w_down[0].astype(BF16), final_norm_w.reshape(1, D), S, _tile(S, 512))
    return out.reshape(B, S, D)
```

```python
import functools
import math

import jax
import jax.numpy as jnp
import numpy as np
from jax import lax
from jax.experimental import pallas as pl
from jax.experimental.pallas import tpu as pltpu

F32 = jnp.float32
BF16 = jnp.bfloat16

D_MODEL = 1024
DA_HEADS = 4
DA_VDIM = 128
DA_QKDIM = 64
DA_WIDTH = DA_HEADS * DA_VDIM
ML_HEADS = 4
ML_DIM = 128
ML_WIDTH = ML_HEADS * ML_DIM
ML_CHUNK = 64
ML_CONV = 4
FFN_CONV = 3
D_FF = 2816
NORM_EPS = 1e-6
LAM_INIT = 0.8 - 0.6 * math.exp(-0.3 * 0)
N_GATES = 2 * ML_HEADS
N_GROUPS = 2 * DA_HEADS
LOG2E = math.log2(math.e)
ALIBI_SLOPES_LOG2 = tuple(float(np.float32(2.0 ** (-8.0 * (h + 1) / DA_HEADS) * LOG2E)) for h in range(DA_HEADS))

LANES = 128
V_ROWS = DA_VDIM + 16
ATT_TILE = 512
HALO = 16
NEG_BIG = -1e30
VMEM_LIMIT = 56 * 1024 * 1024

MAIN_COLS = 2 * DA_WIDTH + DA_WIDTH + 2 * ML_WIDTH + ML_WIDTH + ML_WIDTH


def _params(*sem):
    return pltpu.CompilerParams(dimension_semantics=sem, vmem_limit_bytes=VMEM_LIMIT)


def _sigmoid(x):
    return 0.5 + 0.5 * jnp.tanh(0.5 * x)


def _rms(x, w):
    return x * lax.rsqrt(jnp.mean(x * x, axis=-1, keepdims=True) + NORM_EPS) * w


def _score_aug():
    r = np.arange(ATT_TILE)
    hi = ((r // 16) * 16).astype(np.float32)
    lo = (r % 16).astype(np.float32)
    augk = np.zeros((ATT_TILE, N_GROUPS * LANES), np.float32)
    augq_t = np.zeros((N_GROUPS * LANES, ATT_TILE), np.float32)
    for g in range(N_GROUPS):
        rest = np.float32(ALIBI_SLOPES_LOG2[g // 2])
        parts = []
        for _ in range(3):
            parts.append(np.asarray(rest).astype(BF16).astype(np.float32))
            rest = np.float32(rest - parts[-1])
        assert rest == 0.0
        c0 = g * LANES + (DA_QKDIM if g % 2 == 0 else 0)
        for i, ci in enumerate(parts):
            augk[:, c0 + i] = hi
            augk[:, c0 + 3 + i] = lo
            augk[:, c0 + 6 + i] = -ci
            augk[:, c0 + 9 + i] = -ci
            augq_t[c0 + i] = ci
            augq_t[c0 + 3 + i] = ci
            augq_t[c0 + 6 + i] = hi
            augq_t[c0 + 9 + i] = lo
    return augk, augq_t


def _inproj_kernel(x_ref, xh_ref, nw_ref, wr_ref, wt_ref, augk_ref, augq_ref, cw_ref, cb_ref,
                   qt_ref, ka_ref, vt_ref, mqk_ref, mv_ref, mo_ref, g_ref, *, tiles_per_seq):
    i = pl.program_id(0)
    tm = ATT_TILE
    hn = _rms(x_ref[...], nw_ref[...]).astype(BF16)

    def mm(lo, hi):
        return jnp.dot(hn, wr_ref[:, lo:hi], preferred_element_type=F32)

    def mm_t(lo, hi):
        return lax.dot_general(wt_ref[lo:hi, :], hn, (((1,), (1,)), ((), ())), preferred_element_type=F32)

    qq = mm_t(0, DA_WIDTH) * (DA_QKDIM ** -0.5 * LOG2E)
    kk = mm(0, DA_WIDTH)
    vv = mm_t(DA_WIDTH, 2 * DA_WIDTH)
    one_row = (lax.broadcasted_iota(jnp.int32, (V_ROWS - DA_VDIM, tm), 0) == 0).astype(BF16)
    for h in range(DA_HEADS):
        vt_ref[0, h * V_ROWS:h * V_ROWS + DA_VDIM, :] = vv[h * DA_VDIM:(h + 1) * DA_VDIM, :].astype(BF16)
        vt_ref[0, h * V_ROWS + DA_VDIM:(h + 1) * V_ROWS, :] = one_row
    lane = lax.broadcasted_iota(jnp.int32, (tm, LANES), 1)
    for g in range(N_GROUPS):
        h, c = divmod(g, 2)
        src = h * LANES + c * DA_QKDIM
        dst = g * LANES + c * DA_QKDIM
        spare = g * LANES + (1 - c) * DA_QKDIM
        qt_ref[0, dst:dst + DA_QKDIM, :] = qq[src:src + DA_QKDIM, :].astype(BF16)
        qt_ref[0, spare:spare + DA_QKDIM, :] = augq_ref[spare:spare + DA_QKDIM, :].astype(BF16)
        is_data = (lane < DA_QKDIM) if c == 0 else (lane >= DA_QKDIM)
        ka_ref[:, g * LANES:(g + 1) * LANES] = jnp.where(
            is_data, kk[:, (g // 2) * LANES:(g // 2 + 1) * LANES], augk_ref[:, g * LANES:(g + 1) * LANES]).astype(BF16)

    c = DA_WIDTH
    hh = _rms(xh_ref[...], nw_ref[...]).astype(BF16)
    first = (i % tiles_per_seq == 0)
    for half in range(2):
        lo = c + half * ML_WIDTH
        hist = jnp.dot(hh, wr_ref[:, lo:lo + ML_WIDTH], preferred_element_type=F32)
        hist = jnp.where(first, 0.0, hist)
        pre = jnp.concatenate([hist, mm(lo, lo + ML_WIDTH)], axis=0)
        y = cb_ref[:, half * ML_WIDTH:(half + 1) * ML_WIDTH]
        for t in range(ML_CONV):
            off = HALO - (ML_CONV - 1) + t
            y = y + cw_ref[t:t + 1, half * ML_WIDTH:(half + 1) * ML_WIDTH] * pre[off:off + tm, :]
        y = y * _sigmoid(y)
        if half == 1:
            y = y * (ML_DIM ** -0.5)
        mqk_ref[:, half * ML_WIDTH:(half + 1) * ML_WIDTH] = y.astype(BF16)
    c += 2 * ML_WIDTH
    mv_ref[...] = mm(c, c + 512).astype(BF16); c += 512
    mo_ref[...] = mm(c, c + 512).astype(BF16); c += 512
    g_ref[...] = mm(c, c + LANES)[:, :N_GATES]


def _inproj(x2, nw, w_row, w_t, augk, augq_t, cw, cb, S):
    T = x2.shape[0]
    tm = ATT_TILE
    nt = T // tm
    row = lambda i: (i, 0)
    row3 = lambda i: (i, 0, 0)
    halo = lambda i: (jnp.maximum(i * (tm // HALO) - 1, 0), 0)
    const = lambda i: (0, 0)
    bf = lambda n: jax.ShapeDtypeStruct((T, n), BF16)
    ga = N_GROUPS * LANES
    return pl.pallas_call(
        functools.partial(_inproj_kernel, tiles_per_seq=S // tm),
        grid=(nt,),
        in_specs=[pl.BlockSpec((tm, D_MODEL), row),
                  pl.BlockSpec((HALO, D_MODEL), halo),
                  pl.BlockSpec((1, D_MODEL), const),
                  pl.BlockSpec(w_row.shape, const),
                  pl.BlockSpec(w_t.shape, const),
                  pl.BlockSpec((tm, ga), const),
                  pl.BlockSpec((ga, tm), const),
                  pl.BlockSpec((ML_CONV, 2 * ML_WIDTH), const),
                  pl.BlockSpec((1, 2 * ML_WIDTH), const)],
        out_specs=[pl.BlockSpec((1, ga, tm), row3), pl.BlockSpec((tm, ga), row),
                   pl.BlockSpec((1, DA_HEADS * V_ROWS, tm), row3), pl.BlockSpec((tm, 1024), row),
                   pl.BlockSpec((tm, 512), row), pl.BlockSpec((tm, 512), row),
                   pl.BlockSpec((tm, N_GATES), row)],
        out_shape=[jax.ShapeDtypeStruct((nt, ga, tm), BF16), bf(ga),
                   jax.ShapeDtypeStruct((nt, DA_HEADS * V_ROWS, tm), BF16), bf(1024), bf(512), bf(512),
                   jax.ShapeDtypeStruct((T, N_GATES), F32)],
        compiler_params=_params("parallel"),
        name="inproj",
    )(x2, x2, nw, w_row, w_t, augk, augq_t, cw, cb)


def _attn_kernel(qt_ref, ka_ref, vt_ref, lam_ref, nw_ref, o_ref, s_sc, acc_sc):
    t = ATT_TILE
    qi = pl.program_id(1)
    lv = lam_ref[...]
    lam = (jnp.exp(jnp.sum(lv[0:1] * lv[1:2], axis=-1, keepdims=True))
           - jnp.exp(jnp.sum(lv[2:3] * lv[3:4], axis=-1, keepdims=True)) + LAM_INIT)
    keep = lax.broadcasted_iota(jnp.int32, (t, t), 0) <= lax.broadcasted_iota(jnp.int32, (t, t), 1)

    def scores(h, ki, masked):
        k0 = pl.multiple_of(ki * t, t)
        mts = []
        for c in range(2):
            g = 2 * h + c
            s = jnp.dot(ka_ref[pl.ds(k0, t), g * LANES:(g + 1) * LANES], qt_ref[0, g * LANES:(g + 1) * LANES, :],
                        preferred_element_type=F32)
            if masked:
                s = jnp.where(keep, s, NEG_BIG)
            s_sc[g] = s
            mts.append(jnp.max(s, axis=0, keepdims=True))
        return mts

    def accumulate(h, ki, mts, stats):
        vt = vt_ref[ki, h * V_ROWS:(h + 1) * V_ROWS, :]
        shift = (ALIBI_SLOPES_LOG2[h] * t) * (ki - qi).astype(F32)
        out = []
        for c in range(2):
            g = 2 * h + c
            m_old = stats[c]
            m_new = jnp.maximum(m_old, mts[c] + shift)
            p = jnp.exp2(s_sc[g] - (m_new - shift))
            acc_sc[g] = jnp.exp2(m_old - m_new) * acc_sc[g] + jnp.dot(vt, p.astype(BF16), preferred_element_type=F32)
            out.append(m_new)
        return out

    def trip(ki, carry):
        prev = carry[0]
        out = [ki]
        for h in range(DA_HEADS):
            ch = carry[1 + 4 * h:5 + 4 * h]
            stats = accumulate(h, prev, ch[:2], ch[2:])
            out += scores(h, ki, False) + stats
        return tuple(out)

    acc_sc[...] = jnp.zeros(acc_sc.shape, F32)
    neg = jnp.full((1, t), NEG_BIG, F32)
    init = [qi]
    for h in range(DA_HEADS):
        init += scores(h, qi, True) + [neg, neg]
    carry = lax.fori_loop(0, qi, trip, tuple(init))
    for h in range(DA_HEADS):
        ch = carry[1 + 4 * h:5 + 4 * h]
        accumulate(h, carry[0], ch[:2], ch[2:])
        a0, a1 = acc_sc[2 * h], acc_sc[2 * h + 1]
        o = (a0[:DA_VDIM] / a0[DA_VDIM:DA_VDIM + 1]
             - lam * (a1[:DA_VDIM] / a1[DA_VDIM:DA_VDIM + 1]))
        o = o * lax.rsqrt(jnp.mean(o * o, axis=0, keepdims=True) + NORM_EPS) * nw_ref[...] * (1.0 - LAM_INIT)
        o_ref[:, h * DA_VDIM:(h + 1) * DA_VDIM] = o.T.astype(BF16)


def _attn(qt, ka, vt, lam_vecs, nw_col, B, S):
    t = ATT_TILE
    nq = S // t
    ga = N_GROUPS * LANES
    return pl.pallas_call(
        _attn_kernel,
        grid=(B, nq),
        in_specs=[pl.BlockSpec((1, ga, t), lambda b, i: (b * nq + i, 0, 0)),
                  pl.BlockSpec((S, ga), lambda b, i: (b, 0)),
                  pl.BlockSpec((nq, DA_HEADS * V_ROWS, t), lambda b, i: (b, 0, 0)),
                  pl.BlockSpec((4, DA_QKDIM), lambda b, i: (0, 0)),
                  pl.BlockSpec((DA_VDIM, 1), lambda b, i: (0, 0))],
        out_specs=pl.BlockSpec((t, DA_WIDTH), lambda b, i: (b * nq + i, 0)),
        out_shape=jax.ShapeDtypeStruct((B * S, DA_WIDTH), BF16),
        scratch_shapes=[pltpu.VMEM((N_GROUPS, t, t), F32), pltpu.VMEM((N_GROUPS, V_ROWS, t), F32)],
        compiler_params=_params("parallel", "arbitrary"),
        name="attn",
    )(qt, ka, vt, lam_vecs, nw_col)


def _log_sigmoid(x):
    return jnp.minimum(x, 0.0) - jnp.log(1.0 + jnp.exp(-jnp.abs(x)))


def _mlstm_kernel(qk_sc, v_ref, o_ref, g_ref, gt_ref, gbr_ref, gbc_ref,
                  nw_ref, out_ref, c_sc, m_sc, cprev_sc, mprev_sc, *, ts):
    j = pl.program_id(1)
    L = ML_CHUNK

    @pl.when(j == 0)
    def _():
        c_sc[...] = jnp.zeros(c_sc.shape, F32)
        m_sc[...] = jnp.zeros(m_sc.shape, F32)

    tt = lax.broadcasted_iota(jnp.int32, (L, L), 0)
    ss = lax.broadcasted_iota(jnp.int32, (L, L), 1)
    causal = ss <= tt
    nc = ts // L
    lane = lax.broadcasted_iota(jnp.int32, (nc, L, ML_DIM), 2)
    ones_col = jnp.where(lane == 0, 1.0, 0.0).astype(BF16)
    g = g_ref[...] + gbr_ref[...]
    gt = gt_ref[...] + gbc_ref[...]
    lf, lft = _log_sigmoid(g), _log_sigmoid(gt)
    bdot = lambda x, y, cx, cy: lax.dot_general(x, y, (((cx,), (cy,)), ((0,), (0,))), preferred_element_type=F32)

    for h in range(ML_HEADS):
        hs = slice(h * ML_DIM, (h + 1) * ML_DIM)
        i_col, f_col = g[:, :, h:h + 1], lf[:, :, ML_HEADS + h:ML_HEADS + h + 1]
        i_row, f_row = gt[:, h:h + 1, :], lft[:, ML_HEADS + h:ML_HEADS + h + 1, :]
        b_col = jnp.sum(jnp.where(causal, f_row, 0.0), axis=2, keepdims=True)
        b_row = jnp.sum(jnp.where(tt <= ss, f_col, 0.0), axis=1, keepdims=True)
        log_d = jnp.where(causal, b_col - b_row + i_row, NEG_BIG)
        a = jnp.max(log_d, axis=2, keepdims=True)
        d_loc = jnp.exp(log_d - a)

        q3 = qk_sc[:, hs].reshape(nc, L, ML_DIM)
        k3 = qk_sc[:, ML_WIDTH + h * ML_DIM:ML_WIDTH + (h + 1) * ML_DIM].reshape(nc, L, ML_DIM)
        v_aug = jnp.concatenate([v_ref[:, hs].reshape(nc, L, ML_DIM), ones_col], axis=2)

        sc = bdot(q3, k3, 2, 2) * d_loc
        intra = bdot(sc.astype(BF16), v_aug, 2, 1)
        b_last = b_col[:, L - 1:L, :]
        a_last = a[:, L - 1:L, :]
        w_loc = jnp.exp(b_last - b_col + i_col - a_last)
        wv = (w_loc * v_aug.astype(F32)).astype(BF16)
        upd = [lax.dot_general(k3[ci], wv[ci], (((0,), (0,)), ((), ())), preferred_element_type=F32)
               for ci in range(nc)]

        c_old = c_sc[h]
        m_old = m_sc[h]
        for ci in range(nc):
            cprev_sc[ci] = c_old.astype(BF16)
            mprev_sc[ci] = m_old
            m_new = jnp.maximum(b_last[ci] + m_old, a_last[ci])
            c_old = jnp.exp(b_last[ci] + m_old - m_new) * c_old + jnp.exp(a_last[ci] - m_new) * upd[ci]
            m_old = m_new
        c_sc[h] = c_old
        m_sc[h] = m_old

        log_prev = b_col + mprev_sc[...]
        m_t = jnp.maximum(log_prev, a)
        inter = bdot(q3, cprev_sc[...], 2, 1)
        tot = jnp.exp(a - m_t) * intra + jnp.exp(log_prev - m_t) * inter
        hh = tot[:, :, :ML_DIM] / jnp.maximum(jnp.abs(tot[:, :, ML_DIM:ML_DIM + 1]), jnp.exp(-m_t))
        hh = _rms(hh, nw_ref[:, hs]).reshape(ts, ML_DIM)
        gate = o_ref[:, hs].astype(F32)
        out_ref[:, hs] = (hh * _sigmoid(gate)).astype(BF16)


def _mlstm(mqk, mv, mo, g3, gt3, gb_row, gb_col, nw, B, S, ts):
    ns = S // ts
    cpb = ts // ML_CHUNK
    row = lambda b, j: (b * ns + j, 0)
    ch = lambda b, j: (b * ns + j, 0, 0)
    const = lambda b, j: (0, 0)
    return pl.pallas_call(
        functools.partial(_mlstm_kernel, ts=ts),
        grid=(B, ns),
        in_specs=[pl.BlockSpec((ts, 2 * ML_WIDTH), row),
                  pl.BlockSpec((ts, ML_WIDTH), row),
                  pl.BlockSpec((ts, ML_WIDTH), row),
                  pl.BlockSpec((cpb, ML_CHUNK, N_GATES), ch),
                  pl.BlockSpec((cpb, N_GATES, ML_CHUNK), ch),
                  pl.BlockSpec((1, N_GATES), const),
                  pl.BlockSpec((N_GATES, 1), const),
                  pl.BlockSpec((1, ML_WIDTH), const)],
        out_specs=pl.BlockSpec((ts, ML_WIDTH), row),
        out_shape=jax.ShapeDtypeStruct((B * S, ML_WIDTH), BF16),
        scratch_shapes=[pltpu.VMEM((ML_HEADS, ML_DIM, 2 * ML_DIM), F32),
                        pltpu.VMEM((ML_HEADS, 1, 1), F32),
                        pltpu.VMEM((cpb, ML_DIM, 2 * ML_DIM), BF16),
                        pltpu.VMEM((cpb, 1, 1), F32)],
        compiler_params=_params("parallel", "arbitrary"),
        name="mlstm",
    )(mqk, mv, mo, g3, gt3, gb_row, gb_col, nw)


FF_CHUNK = 256


def _mixer_kernel(x_ref, xh_ref, da_ref, dah_ref, ml_ref, mlh_ref, wa_ref, wm_ref, nw_ref, wup_ref, cw_ref,
                  cb_ref, wd_ref, fw_ref, o_ref, x1_sc, hn_sc, act_sc, *, tm, tiles_per_seq):
    i = pl.program_id(0)

    def residual(xr, dr, mr):
        return (xr[...] + jnp.dot(dr[...], wa_ref[...], preferred_element_type=F32)
                + jnp.dot(mr[...], wm_ref[...], preferred_element_type=F32))

    x1 = residual(x_ref, da_ref, ml_ref)
    x1_sc[...] = x1
    hn_sc[HALO:, :] = _rms(x1, nw_ref[...]).astype(BF16)

    @pl.when(i % tiles_per_seq == 0)
    def _():
        hn_sc[0:HALO, :] = jnp.zeros((HALO, D_MODEL), BF16)

    @pl.when(i % tiles_per_seq != 0)
    def _():
        hn_sc[0:HALO, :] = _rms(residual(xh_ref, dah_ref, mlh_ref), nw_ref[...]).astype(BF16)

    nf = D_FF // FF_CHUNK

    def up(j):
        hn = hn_sc[...]
        out = []
        for part in range(2):
            lo = part * D_FF + j * FF_CHUNK
            out.append(jnp.dot(hn, wup_ref[:, lo:lo + FF_CHUNK], preferred_element_type=F32))
        return out

    def activate(j, u):
        y = []
        for part in range(2):
            lo = part * D_FF + j * FF_CHUNK
            acc = cb_ref[:, lo:lo + FF_CHUNK]
            for t in range(FFN_CONV):
                off = HALO - (FFN_CONV - 1) + t
                acc = acc + cw_ref[t:t + 1, lo:lo + FF_CHUNK] * u[part][off:off + tm, :]
            y.append(acc)
        act = y[0] * _sigmoid(y[0]) * y[1]
        act_sc[:, j * FF_CHUNK:(j + 1) * FF_CHUNK] = act.astype(BF16)

    u = up(0)
    for j in range(nf):
        u_next = up(j + 1) if j + 1 < nf else None
        activate(j, u)
        u = u_next

    y = x1_sc[...] + jnp.dot(act_sc[...], wd_ref[...], preferred_element_type=F32)
    o_ref[...] = _rms(y, fw_ref[...])


def _mixer(x2, da, ml, wa, wm, nw, w_up, cw, cb, w_down, fw, S, tm):
    T = x2.shape[0]
    row = lambda i: (i, 0)
    halo = lambda i: (jnp.maximum(i * (tm // HALO) - 1, 0), 0)
    const = lambda i: (0, 0)
    resident = lambda shape: pl.BlockSpec(shape, const, pipeline_mode=pl.Buffered(1))
    return pl.pallas_call(
        functools.partial(_mixer_kernel, tm=tm, tiles_per_seq=S // tm),
        grid=(T // tm,),
        in_specs=[pl.BlockSpec((tm, D_MODEL), row), pl.BlockSpec((HALO, D_MODEL), halo),
                  pl.BlockSpec((tm, DA_WIDTH), row), pl.BlockSpec((HALO, DA_WIDTH), halo),
                  pl.BlockSpec((tm, ML_WIDTH), row), pl.BlockSpec((HALO, ML_WIDTH), halo),
                  resident((DA_WIDTH, D_MODEL)), resident((ML_WIDTH, D_MODEL)),
                  resident((1, D_MODEL)),
                  resident((D_MODEL, 2 * D_FF)),
                  resident((FFN_CONV, 2 * D_FF)),
                  resident((1, 2 * D_FF)),
                  resident((D_FF, D_MODEL)),
                  resident((1, D_MODEL))],
        out_specs=pl.BlockSpec((tm, D_MODEL), row),
        out_shape=jax.ShapeDtypeStruct((T, D_MODEL), F32),
        scratch_shapes=[pltpu.VMEM((tm, D_MODEL), F32),
                        pltpu.VMEM((tm + HALO, D_MODEL), BF16),
                        pltpu.VMEM((tm, D_FF), BF16)],
        compiler_params=_params("parallel"),
        name="mixer",
    )(x2, x2, da, da, ml, ml, wa, wm, nw, w_up, cw, cb, w_down, fw)


def _tile(n, pref):
    t = min(n, pref)
    assert n % t == 0, (n, t)
    return t


def kernel(x, attn_norm_w, w_in, mlstm_conv_w, mlstm_conv_b, mlstm_igate_b, mlstm_fgate_b, lambda_q1, lambda_k1, lambda_q2, lambda_k2, diff_norm_w, mlstm_norm_w, w_out, ffn_norm_w, w_up, ffn_conv_w, ffn_conv_b, w_down, final_norm_w):
    B, S, D = x.shape
    assert D == D_MODEL and S % ATT_TILE == 0 and attn_norm_w.shape[0] == 1
    T = B * S
    x2 = x.reshape(T, D)

    w = w_in[0]

    w_gate = jnp.pad(w[:, MAIN_COLS:], ((0, 0), (0, LANES - N_GATES)))
    w_row = jnp.concatenate([w[:, 512:1024], w[:, 1536:MAIN_COLS], w_gate], axis=1).astype(BF16)
    w_t = jnp.concatenate([w[:, :512], w[:, 1024:1536]], axis=1).T.astype(BF16)
    augk, augq_t = _score_aug()
    qt, ka, vt, mqk, mv, mo, g = _inproj(x2, attn_norm_w, w_row, w_t, jnp.asarray(augk), jnp.asarray(augq_t),
                                         mlstm_conv_w[0], mlstm_conv_b, S)

    lam_vecs = jnp.concatenate([lambda_q1, lambda_k1, lambda_q2, lambda_k2], axis=0).astype(F32)
    da = _attn(qt, ka, vt, lam_vecs, diff_norm_w.reshape(DA_VDIM, 1), B, S)

    nchunks = T // ML_CHUNK
    g3 = g.reshape(nchunks, ML_CHUNK, N_GATES)
    gt3 = jnp.swapaxes(g3, 1, 2)
    gate_b = jnp.concatenate([mlstm_igate_b, mlstm_fgate_b], axis=1).astype(F32)
    ml = _mlstm(mqk, mv, mo, g3, gt3, gate_b, gate_b.reshape(N_GATES, 1), mlstm_norm_w, B, S, _tile(S, 512))

    w_o = w_out[0].astype(BF16)
    out = _mixer(x2, da, ml, w_o[:DA_WIDTH], w_o[DA_WIDTH:], ffn_norm_w, w_up[0].astype(BF16), ffn_conv_w[0],
                 ffn_conv_b, w_down[0].astype(BF16), final_norm_w.reshape(1, D), S, _tile(S, 512))
    return out.reshape(B, S, D)
```

```python
import functools
import math

import jax
import jax.numpy as jnp
import numpy as np
from jax import lax
from jax.experimental import pallas as pl
from jax.experimental.pallas import tpu as pltpu

F32 = jnp.float32
BF16 = jnp.bfloat16

D_MODEL = 1024
DA_HEADS = 4
DA_VDIM = 128
DA_QKDIM = 64
DA_WIDTH = DA_HEADS * DA_VDIM
ML_HEADS = 4
ML_DIM = 128
ML_WIDTH = ML_HEADS * ML_DIM
ML_CHUNK = 64
ML_CONV = 4
FFN_CONV = 3
D_FF = 2816
NORM_EPS = 1e-6
LAM_INIT = 0.8 - 0.6 * math.exp(-0.3 * 0)
N_GATES = 2 * ML_HEADS
N_GROUPS = 2 * DA_HEADS
LOG2E = math.log2(math.e)
ALIBI_SLOPES_LOG2 = tuple(float(np.float32(2.0 ** (-8.0 * (h + 1) / DA_HEADS) * LOG2E)) for h in range(DA_HEADS))

LANES = 128
V_ROWS = DA_VDIM + 16
ATT_TILE = 512
HALO = 16
NEG_BIG = -1e30
VMEM_LIMIT = 56 * 1024 * 1024

MAIN_COLS = 2 * DA_WIDTH + DA_WIDTH + 2 * ML_WIDTH + ML_WIDTH + ML_WIDTH


def _params(*sem):
    return pltpu.CompilerParams(dimension_semantics=sem, vmem_limit_bytes=VMEM_LIMIT)


def _sigmoid(x):
    return 0.5 + 0.5 * jnp.tanh(0.5 * x)


def _rms(x, w):
    return x * lax.rsqrt(jnp.mean(x * x, axis=-1, keepdims=True) + NORM_EPS) * w


def _score_aug():
    r = np.arange(ATT_TILE)
    hi = ((r // 16) * 16).astype(np.float32)
    lo = (r % 16).astype(np.float32)
    augk = np.zeros((ATT_TILE, N_GROUPS * LANES), np.float32)
    augq_t = np.zeros((N_GROUPS * LANES, ATT_TILE), np.float32)
    for g in range(N_GROUPS):
        rest = np.float32(ALIBI_SLOPES_LOG2[g // 2])
        parts = []
        for _ in range(3):
            parts.append(np.asarray(rest).astype(BF16).astype(np.float32))
            rest = np.float32(rest - parts[-1])
        assert rest == 0.0
        c0 = g * LANES + (DA_QKDIM if g % 2 == 0 else 0)
        for i, ci in enumerate(parts):
            augk[:, c0 + i] = hi
            augk[:, c0 + 3 + i] = lo
            augk[:, c0 + 6 + i] = -ci
            augk[:, c0 + 9 + i] = -ci
            augq_t[c0 + i] = ci
            augq_t[c0 + 3 + i] = ci
            augq_t[c0 + 6 + i] = hi
            augq_t[c0 + 9 + i] = lo
    return augk, augq_t


def _inproj_kernel(x_ref, xh_ref, nw_ref, wr_ref, wt_ref, augk_ref, augq_ref, cw_ref, cb_ref,
                   qt_ref, ka_ref, vt_ref, mqk_ref, mv_ref, mo_ref, g_ref, *, tiles_per_seq):
    i = pl.program_id(0)
    tm = ATT_TILE
    hn = _rms(x_ref[...], nw_ref[...]).astype(BF16)

    def mm(lo, hi):
        return jnp.dot(hn, wr_ref[:, lo:hi], preferred_element_type=F32)

    def mm_t(lo, hi):
        return lax.dot_general(wt_ref[lo:hi, :], hn, (((1,), (1,)), ((), ())), preferred_element_type=F32)

    qq = mm_t(0, DA_WIDTH) * (DA_QKDIM ** -0.5 * LOG2E)
    kk = mm(0, DA_WIDTH)
    vv = mm_t(DA_WIDTH, 2 * DA_WIDTH)
    one_row = (lax.broadcasted_iota(jnp.int32, (V_ROWS - DA_VDIM, tm), 0) == 0).astype(BF16)
    for h in range(DA_HEADS):
        vt_ref[0, h * V_ROWS:h * V_ROWS + DA_VDIM, :] = vv[h * DA_VDIM:(h + 1) * DA_VDIM, :].astype(BF16)
        vt_ref[0, h * V_ROWS + DA_VDIM:(h + 1) * V_ROWS, :] = one_row
    lane = lax.broadcasted_iota(jnp.int32, (tm, LANES), 1)
    for g in range(N_GROUPS):
        h, c = divmod(g, 2)
        src = h * LANES + c * DA_QKDIM
        dst = g * LANES + c * DA_QKDIM
        spare = g * LANES + (1 - c) * DA_QKDIM
        qt_ref[0, dst:dst + DA_QKDIM, :] = qq[src:src + DA_QKDIM, :].astype(BF16)
        qt_ref[0, spare:spare + DA_QKDIM, :] = augq_ref[spare:spare + DA_QKDIM, :].astype(BF16)
        is_data = (lane < DA_QKDIM) if c == 0 else (lane >= DA_QKDIM)
        ka_ref[:, g * LANES:(g + 1) * LANES] = jnp.where(
            is_data, kk[:, (g // 2) * LANES:(g // 2 + 1) * LANES], augk_ref[:, g * LANES:(g + 1) * LANES]).astype(BF16)

    c = DA_WIDTH
    hh = _rms(xh_ref[...], nw_ref[...]).astype(BF16)
    first = (i % tiles_per_seq == 0)
    for half in range(2):
        lo = c + half * ML_WIDTH
        hist = jnp.dot(hh, wr_ref[:, lo:lo + ML_WIDTH], preferred_element_type=F32)
        hist = jnp.where(first, 0.0, hist)
        pre = jnp.concatenate([hist, mm(lo, lo + ML_WIDTH)], axis=0)
        y = cb_ref[:, half * ML_WIDTH:(half + 1) * ML_WIDTH]
        for t in range(ML_CONV):
            off = HALO - (ML_CONV - 1) + t
            y = y + cw_ref[t:t + 1, half * ML_WIDTH:(half + 1) * ML_WIDTH] * pre[off:off + tm, :]
        y = y * _sigmoid(y)
        if half == 1:
            y = y * (ML_DIM ** -0.5)
        mqk_ref[:, half * ML_WIDTH:(half + 1) * ML_WIDTH] = y.astype(BF16)
    c += 2 * ML_WIDTH
    mv_ref[...] = mm(c, c + 512).astype(BF16); c += 512
    mo_ref[...] = mm(c, c + 512).astype(BF16); c += 512
    g_ref[...] = mm(c, c + LANES)[:, :N_GATES]


def _inproj(x2, nw, w_row, w_t, augk, augq_t, cw, cb, S):
    T = x2.shape[0]
    tm = ATT_TILE
    nt = T // tm
    row = lambda i: (i, 0)
    row3 = lambda i: (i, 0, 0)
    halo = lambda i: (jnp.maximum(i * (tm // HALO) - 1, 0), 0)
    const = lambda i: (0, 0)
    bf = lambda n: jax.ShapeDtypeStruct((T, n), BF16)
    ga = N_GROUPS * LANES
    return pl.pallas_call(
        functools.partial(_inproj_kernel, tiles_per_seq=S // tm),
        grid=(nt,),
        in_specs=[pl.BlockSpec((tm, D_MODEL), row),
                  pl.BlockSpec((HALO, D_MODEL), halo),
                  pl.BlockSpec((1, D_MODEL), const),
                  pl.BlockSpec(w_row.shape, const),
                  pl.BlockSpec(w_t.shape, const),
                  pl.BlockSpec((tm, ga), const),
                  pl.BlockSpec((ga, tm), const),
                  pl.BlockSpec((ML_CONV, 2 * ML_WIDTH), const),
                  pl.BlockSpec((1, 2 * ML_WIDTH), const)],
        out_specs=[pl.BlockSpec((1, ga, tm), row3), pl.BlockSpec((tm, ga), row),
                   pl.BlockSpec((1, DA_HEADS * V_ROWS, tm), row3), pl.BlockSpec((tm, 1024), row),
                   pl.BlockSpec((tm, 512), row), pl.BlockSpec((tm, 512), row),
                   pl.BlockSpec((tm, N_GATES), row)],
        out_shape=[jax.ShapeDtypeStruct((nt, ga, tm), BF16), bf(ga),
                   jax.ShapeDtypeStruct((nt, DA_HEADS * V_ROWS, tm), BF16), bf(1024), bf(512), bf(512),
                   jax.ShapeDtypeStruct((T, N_GATES), F32)],
        compiler_params=_params("parallel"),
        name="inproj",
    )(x2, x2, nw, w_row, w_t, augk, augq_t, cw, cb)


def _attn_kernel(qt_ref, ka_ref, vt_ref, lam_ref, nw_ref, o_ref, s_sc, acc_sc):
    t = ATT_TILE
    qi = pl.program_id(1)
    lv = lam_ref[...]
    lam = (jnp.exp(jnp.sum(lv[0:1] * lv[1:2], axis=-1, keepdims=True))
           - jnp.exp(jnp.sum(lv[2:3] * lv[3:4], axis=-1, keepdims=True)) + LAM_INIT)
    keep = lax.broadcasted_iota(jnp.int32, (t, t), 0) <= lax.broadcasted_iota(jnp.int32, (t, t), 1)

    def scores(h, ki, masked):
        k0 = pl.multiple_of(ki * t, t)
        mts = []
        for c in range(2):
            g = 2 * h + c
            s = jnp.dot(ka_ref[pl.ds(k0, t), g * LANES:(g + 1) * LANES], qt_ref[0, g * LANES:(g + 1) * LANES, :],
                        preferred_element_type=F32)
            if masked:
                s = jnp.where(keep, s, NEG_BIG)
            s_sc[g] = s
            mts.append(jnp.max(s, axis=0, keepdims=True))
        return mts

    def accumulate(h, ki, mts, stats):
        vt = vt_ref[ki, h * V_ROWS:(h + 1) * V_ROWS, :]
        shift = (ALIBI_SLOPES_LOG2[h] * t) * (ki - qi).astype(F32)
        out = []
        for c in range(2):
            g = 2 * h + c
            m_old = stats[c]
            m_new = jnp.maximum(m_old, mts[c] + shift)
            p = jnp.exp2(s_sc[g] - (m_new - shift))
            acc_sc[g] = jnp.exp2(m_old - m_new) * acc_sc[g] + jnp.dot(vt, p.astype(BF16), preferred_element_type=F32)
            out.append(m_new)
        return out

    def trip(ki, carry):
        prev = carry[0]
        out = [ki]
        for h in range(DA_HEADS):
            ch = carry[1 + 4 * h:5 + 4 * h]
            stats = accumulate(h, prev, ch[:2], ch[2:])
            out += scores(h, ki, False) + stats
        return tuple(out)

    acc_sc[...] = jnp.zeros(acc_sc.shape, F32)
    neg = jnp.full((1, t), NEG_BIG, F32)
    init = [qi]
    for h in range(DA_HEADS):
        init += scores(h, qi, True) + [neg, neg]
    carry = lax.fori_loop(0, qi, trip, tuple(init))
    for h in range(DA_HEADS):
        ch = carry[1 + 4 * h:5 + 4 * h]
        accumulate(h, carry[0], ch[:2], ch[2:])
        a0, a1 = acc_sc[2 * h], acc_sc[2 * h + 1]
        o = (a0[:DA_VDIM] / a0[DA_VDIM:DA_VDIM + 1]
             - lam * (a1[:DA_VDIM] / a1[DA_VDIM:DA_VDIM + 1]))
        o = o * lax.rsqrt(jnp.mean(o * o, axis=0, keepdims=True) + NORM_EPS) * nw_ref[...] * (1.0 - LAM_INIT)
        o_ref[:, h * DA_VDIM:(h + 1) * DA_VDIM] = o.T.astype(BF16)


def _attn(qt, ka, vt, lam_vecs, nw_col, B, S):
    t = ATT_TILE
    nq = S // t
    ga = N_GROUPS * LANES
    return pl.pallas_call(
        _attn_kernel,
        grid=(B, nq),
        in_specs=[pl.BlockSpec((1, ga, t), lambda b, i: (b * nq + i, 0, 0)),
                  pl.BlockSpec((S, ga), lambda b, i: (b, 0)),
                  pl.BlockSpec((nq, DA_HEADS * V_ROWS, t), lambda b, i: (b, 0, 0)),
                  pl.BlockSpec((4, DA_QKDIM), lambda b, i: (0, 0)),
                  pl.BlockSpec((DA_VDIM, 1), lambda b, i: (0, 0))],
        out_specs=pl.BlockSpec((t, DA_WIDTH), lambda b, i: (b * nq + i, 0)),
        out_shape=jax.ShapeDtypeStruct((B * S, DA_WIDTH), BF16),
        scratch_shapes=[pltpu.VMEM((N_GROUPS, t, t), F32), pltpu.VMEM((N_GROUPS, V_ROWS, t), F32)],
        compiler_params=_params("parallel", "arbitrary"),
        name="attn",
    )(qt, ka, vt, lam_vecs, nw_col)


def _log_sigmoid(x):
    return jnp.minimum(x, 0.0) - jnp.log(1.0 + jnp.exp(-jnp.abs(x)))


def _mlstm_kernel(qk_sc, v_ref, o_ref, g_ref, gt_ref, gbr_ref, gbc_ref,
                  nw_ref, out_ref, c_sc, m_sc, cprev_sc, mprev_sc, *, ts):
    j = pl.program_id(1)
    L = ML_CHUNK

    @pl.when(j == 0)
    def _():
        c_sc[...] = jnp.zeros(c_sc.shape, F32)
        m_sc[...] = jnp.zeros(m_sc.shape, F32)

    tt = lax.broadcasted_iota(jnp.int32, (L, L), 0)
    ss = lax.broadcasted_iota(jnp.int32, (L, L), 1)
    causal = ss <= tt
    nc = ts // L
    lane = lax.broadcasted_iota(jnp.int32, (nc, L, ML_DIM), 2)
    ones_col = jnp.where(lane == 0, 1.0, 0.0).astype(BF16)
    g = g_ref[...] + gbr_ref[...]
    gt = gt_ref[...] + gbc_ref[...]
    lf, lft = _log_sigmoid(g), _log_sigmoid(gt)
    bdot = lambda x, y, cx, cy: lax.dot_general(x, y, (((cx,), (cy,)), ((0,), (0,))), preferred_element_type=F32)

    for h in range(ML_HEADS):
        hs = slice(h * ML_DIM, (h + 1) * ML_DIM)
        i_col, f_col = g[:, :, h:h + 1], lf[:, :, ML_HEADS + h:ML_HEADS + h + 1]
        i_row, f_row = gt[:, h:h + 1, :], lft[:, ML_HEADS + h:ML_HEADS + h + 1, :]
        b_col = jnp.sum(jnp.where(causal, f_row, 0.0), axis=2, keepdims=True)
        b_row = jnp.sum(jnp.where(tt <= ss, f_col, 0.0), axis=1, keepdims=True)
        log_d = jnp.where(causal, b_col - b_row + i_row, NEG_BIG)
        a = jnp.max(log_d, axis=2, keepdims=True)
        d_loc = jnp.exp(log_d - a)

        q3 = qk_sc[:, hs].reshape(nc, L, ML_DIM)
        k3 = qk_sc[:, ML_WIDTH + h * ML_DIM:ML_WIDTH + (h + 1) * ML_DIM].reshape(nc, L, ML_DIM)
        v_aug = jnp.concatenate([v_ref[:, hs].reshape(nc, L, ML_DIM), ones_col], axis=2)

        sc = bdot(q3, k3, 2, 2) * d_loc
        intra = bdot(sc.astype(BF16), v_aug, 2, 1)
        b_last = b_col[:, L - 1:L, :]
        a_last = a[:, L - 1:L, :]
        w_loc = jnp.exp(b_last - b_col + i_col - a_last)
        wv = (w_loc * v_aug.astype(F32)).astype(BF16)
        upd = [lax.dot_general(k3[ci], wv[ci], (((0,), (0,)), ((), ())), preferred_element_type=F32)
               for ci in range(nc)]

        c_old = c_sc[h]
        m_old = m_sc[h]
        for ci in range(nc):
            cprev_sc[ci] = c_old.astype(BF16)
            mprev_sc[ci] = m_old
            m_new = jnp.maximum(b_last[ci] + m_old, a_last[ci])
            c_old = jnp.exp(b_last[ci] + m_old - m_new) * c_old + jnp.exp(a_last[ci] - m_new) * upd[ci]
            m_old = m_new
        c_sc[h] = c_old
        m_sc[h] = m_old

        log_prev = b_col + mprev_sc[...]
        m_t = jnp.maximum(log_prev, a)
        inter = bdot(q3, cprev_sc[...], 2, 1)
        tot = jnp.exp(a - m_t) * intra + jnp.exp(log_prev - m_t) * inter
        hh = tot[:, :, :ML_DIM] / jnp.maximum(jnp.abs(tot[:, :, ML_DIM:ML_DIM + 1]), jnp.exp(-m_t))
        hh = _rms(hh, nw_ref[:, hs]).reshape(ts, ML_DIM)
        gate = o_ref[:, hs].astype(F32)
        out_ref[:, hs] = (hh * _sigmoid(gate)).astype(BF16)


def _mlstm(mqk, mv, mo, g3, gt3, gb_row, gb_col, nw, B, S, ts):
    ns = S // ts
    cpb = ts // ML_CHUNK
    row = lambda b, j: (b * ns + j, 0)
    ch = lambda b, j: (b * ns + j, 0, 0)
    const = lambda b, j: (0, 0)
    return pl.pallas_call(
        functools.partial(_mlstm_kernel, ts=ts),
        grid=(B, ns),
        in_specs=[pl.BlockSpec((ts, 2 * ML_WIDTH), row),
                  pl.BlockSpec((ts, ML_WIDTH), row),
                  pl.BlockSpec((ts, ML_WIDTH), row),
                  pl.BlockSpec((cpb, ML_CHUNK, N_GATES), ch),
                  pl.BlockSpec((cpb, N_GATES, ML_CHUNK), ch),
                  pl.BlockSpec((1, N_GATES), const),
                  pl.BlockSpec((N_GATES, 1), const),
                  pl.BlockSpec((1, ML_WIDTH), const)],
        out_specs=pl.BlockSpec((ts, ML_WIDTH), row),
        out_shape=jax.ShapeDtypeStruct((B * S, ML_WIDTH), BF16),
        scratch_shapes=[pltpu.VMEM((ML_HEADS, ML_DIM, 2 * ML_DIM), F32),
                        pltpu.VMEM((ML_HEADS, 1, 1), F32),
                        pltpu.VMEM((cpb, ML_DIM, 2 * ML_DIM), BF16),
                        pltpu.VMEM((cpb, 1, 1), F32)],
        compiler_params=_params("parallel", "arbitrary"),
        name="mlstm",
    )(mqk, mv, mo, g3, gt3, gb_row, gb_col, nw)


FF_CHUNK = 256


def _mixer_kernel(x0_ref, da0_ref, ml0_ref, xn_ref, dan_ref, mln_ref, wa_ref, wm_ref, nw_ref, wup_ref, cw_ref,
                  cb_ref, wd_ref, fw_ref, o_ref, x1_sc, x1n_sc, hn_sc, u_sc, act_sc, *, tm, tiles_per_seq):
    i = pl.program_id(0)

    def prepare(xr, dr, mr):
        x1 = (xr[...] + jnp.dot(dr[...], wa_ref[...], preferred_element_type=F32)
              + jnp.dot(mr[...], wm_ref[...], preferred_element_type=F32))
        x1n_sc[...] = x1
        hn_sc[HALO:, :] = _rms(x1, nw_ref[...]).astype(BF16)

    @pl.when(i == 0)
    def _():
        prepare(x0_ref, da0_ref, ml0_ref)
        hn_sc[0:HALO, :] = jnp.zeros((HALO, D_MODEL), BF16)

    x1_sc[...] = x1n_sc[...]
    nf = D_FF // FF_CHUNK

    def up(j):
        hn = hn_sc[...]
        for part in range(2):
            lo = part * D_FF + j * FF_CHUNK
            u_sc[j % 2, part] = jnp.dot(hn, wup_ref[:, lo:lo + FF_CHUNK], preferred_element_type=F32)

    def activate(j):
        y = []
        for part in range(2):
            lo = part * D_FF + j * FF_CHUNK
            acc = cb_ref[:, lo:lo + FF_CHUNK]
            for t in range(FFN_CONV):
                off = HALO - (FFN_CONV - 1) + t
                acc = acc + cw_ref[t:t + 1, lo:lo + FF_CHUNK] * u_sc[j % 2, part, off:off + tm, :]
            y.append(acc)
        act = y[0] * _sigmoid(y[0]) * y[1]
        act_sc[:, j * FF_CHUNK:(j + 1) * FF_CHUNK] = act.astype(BF16)

    up(0)
    for j in range(nf):
        if j + 1 < nf:
            up(j + 1)
        activate(j)

    hist = _rms(x1_sc[tm - HALO:tm, :], nw_ref[...])
    hn_sc[0:HALO, :] = jnp.where((i + 1) % tiles_per_seq == 0, 0.0, hist).astype(BF16)
    prepare(xn_ref, dan_ref, mln_ref)

    y = x1_sc[...] + jnp.dot(act_sc[...], wd_ref[...], preferred_element_type=F32)
    o_ref[...] = _rms(y, fw_ref[...])


def _mixer(x2, da, ml, wa, wm, nw, w_up, cw, cb, w_down, fw, S, tm):
    T = x2.shape[0]
    n = T // tm
    row = lambda i: (i, 0)
    first = lambda i: (0, 0)
    nxt = lambda i: (jnp.minimum(i + 1, n - 1), 0)
    const = lambda i: (0, 0)
    resident = lambda shape: pl.BlockSpec(shape, const, pipeline_mode=pl.Buffered(1))
    return pl.pallas_call(
        functools.partial(_mixer_kernel, tm=tm, tiles_per_seq=S // tm),
        grid=(n,),
        in_specs=[pl.BlockSpec((tm, D_MODEL), first), pl.BlockSpec((tm, DA_WIDTH), first),
                  pl.BlockSpec((tm, ML_WIDTH), first),
                  pl.BlockSpec((tm, D_MODEL), nxt), pl.BlockSpec((tm, DA_WIDTH), nxt),
                  pl.BlockSpec((tm, ML_WIDTH), nxt),
                  resident((DA_WIDTH, D_MODEL)), resident((ML_WIDTH, D_MODEL)),
                  resident((1, D_MODEL)),
                  resident((D_MODEL, 2 * D_FF)),
                  resident((FFN_CONV, 2 * D_FF)),
                  resident((1, 2 * D_FF)),
                  resident((D_FF, D_MODEL)),
                  resident((1, D_MODEL))],
        out_specs=pl.BlockSpec((tm, D_MODEL), row),
        out_shape=jax.ShapeDtypeStruct((T, D_MODEL), F32),
        scratch_shapes=[pltpu.VMEM((tm, D_MODEL), F32),
                        pltpu.VMEM((tm, D_MODEL), F32),
                        pltpu.VMEM((tm + HALO, D_MODEL), BF16),
                        pltpu.VMEM((2, 2, tm + HALO, FF_CHUNK), F32),
                        pltpu.VMEM((tm, D_FF), BF16)],
        compiler_params=_params("arbitrary"),
        name="mixer",
    )(x2, da, ml, x2, da, ml, wa, wm, nw, w_up, cw, cb, w_down, fw)


def _tile(n, pref):
    t = min(n, pref)
    assert n % t == 0, (n, t)
    return t


def kernel(x, attn_norm_w, w_in, mlstm_conv_w, mlstm_conv_b, mlstm_igate_b, mlstm_fgate_b, lambda_q1, lambda_k1, lambda_q2, lambda_k2, diff_norm_w, mlstm_norm_w, w_out, ffn_norm_w, w_up, ffn_conv_w, ffn_conv_b, w_down, final_norm_w):
    B, S, D = x.shape
    assert D == D_MODEL and S % ATT_TILE == 0 and attn_norm_w.shape[0] == 1
    T = B * S
    x2 = x.reshape(T, D)

    w = w_in[0]

    w_gate = jnp.pad(w[:, MAIN_COLS:], ((0, 0), (0, LANES - N_GATES)))
    w_row = jnp.concatenate([w[:, 512:1024], w[:, 1536:MAIN_COLS], w_gate], axis=1).astype(BF16)
    w_t = jnp.concatenate([w[:, :512], w[:, 1024:1536]], axis=1).T.astype(BF16)
    augk, augq_t = _score_aug()
    qt, ka, vt, mqk, mv, mo, g = _inproj(x2, attn_norm_w, w_row, w_t, jnp.asarray(augk), jnp.asarray(augq_t),
                                         mlstm_conv_w[0], mlstm_conv_b, S)

    lam_vecs = jnp.concatenate([lambda_q1, lambda_k1, lambda_q2, lambda_k2], axis=0).astype(F32)
    da = _attn(qt, ka, vt, lam_vecs, diff_norm_w.reshape(DA_VDIM, 1), B, S)

    nchunks = T // ML_CHUNK
    g3 = g.reshape(nchunks, ML_CHUNK, N_GATES)
    gt3 = jnp.swapaxes(g3, 1, 2)
    gate_b = jnp.concatenate([mlstm_igate_b, mlstm_fgate_b], axis=1).astype(F32)
    ml = _mlstm(mqk, mv, mo, g3, gt3, gate_b, gate_b.reshape(N_GATES, 1), mlstm_norm_w, B, S, _tile(S, 1024))

    w_o = w_out[0].astype(BF16)
    out = _mixer(x2, da, ml, w_o[:DA_WIDTH], w_o[DA_WIDTH:], ffn_norm_w, w_up[0].astype(BF16), ffn_conv_w[0],
                 ffn_conv_b, w_down[0].astype(BF16), final_norm_w.reshape(1, D), S, _tile(S, 512))
    return out.reshape(B, S, D)
```

```python
import functools
import math

import jax
import jax.numpy as jnp
import numpy as np
from jax import lax
from jax.experimental import pallas as pl
from jax.experimental.pallas import tpu as pltpu

F32 = jnp.float32
BF16 = jnp.bfloat16

D_MODEL = 1024
DA_HEADS = 4
DA_VDIM = 128
DA_QKDIM = 64
DA_WIDTH = DA_HEADS * DA_VDIM
ML_HEADS = 4
ML_DIM = 128
ML_WIDTH = ML_HEADS * ML_DIM
ML_CHUNK = 64
ML_CONV = 4
FFN_CONV = 3
D_FF = 2816
NORM_EPS = 1e-6
LAM_INIT = 0.8 - 0.6 * math.exp(-0.3 * 0)
N_GATES = 2 * ML_HEADS
N_GROUPS = 2 * DA_HEADS
LOG2E = math.log2(math.e)
ALIBI_SLOPES_LOG2 = tuple(float(np.float32(2.0 ** (-8.0 * (h + 1) / DA_HEADS) * LOG2E)) for h in range(DA_HEADS))

LANES = 128
V_ROWS = DA_VDIM + 16
ATT_TILE = 512
HALO = 16
NEG_BIG = -1e30
VMEM_LIMIT = 56 * 1024 * 1024

MAIN_COLS = 2 * DA_WIDTH + DA_WIDTH + 2 * ML_WIDTH + ML_WIDTH + ML_WIDTH


def _params(*sem):
    return pltpu.CompilerParams(dimension_semantics=sem, vmem_limit_bytes=VMEM_LIMIT)


def _sigmoid(x):
    return 0.5 + 0.5 * jnp.tanh(0.5 * x)


def _rms(x, w):
    return x * lax.rsqrt(jnp.mean(x * x, axis=-1, keepdims=True) + NORM_EPS) * w


def _score_aug():
    r = np.arange(ATT_TILE)
    hi = ((r // 16) * 16).astype(np.float32)
    lo = (r % 16).astype(np.float32)
    augk = np.zeros((ATT_TILE, N_GROUPS * LANES), np.float32)
    augq_t = np.zeros((N_GROUPS * LANES, ATT_TILE), np.float32)
    for g in range(N_GROUPS):
        rest = np.float32(ALIBI_SLOPES_LOG2[g // 2])
        parts = []
        for _ in range(3):
            parts.append(np.asarray(rest).astype(BF16).astype(np.float32))
            rest = np.float32(rest - parts[-1])
        assert rest == 0.0
        c0 = g * LANES + (DA_QKDIM if g % 2 == 0 else 0)
        for i, ci in enumerate(parts):
            augk[:, c0 + i] = hi
            augk[:, c0 + 3 + i] = lo
            augk[:, c0 + 6 + i] = -ci
            augk[:, c0 + 9 + i] = -ci
            augq_t[c0 + i] = ci
            augq_t[c0 + 3 + i] = ci
            augq_t[c0 + 6 + i] = hi
            augq_t[c0 + 9 + i] = lo
    return augk, augq_t


SUB_TILES = 2


def _inproj_kernel(x_ref, xh_ref, nw_ref, wr_ref, wt_ref, augk_ref, augq_ref, cw_ref, cb_ref,
                   qt_ref, ka_ref, vt_ref, mqk_ref, mv_ref, mo_ref, g_ref, *, tiles_per_seq):
    i = pl.program_id(0)
    tm = ATT_TILE
    lane = lax.broadcasted_iota(jnp.int32, (tm, LANES), 1)
    one_row = (lax.broadcasted_iota(jnp.int32, (V_ROWS - DA_VDIM, tm), 0) == 0).astype(BF16)

    for sub in range(SUB_TILES):
        rows = slice(sub * tm, (sub + 1) * tm)
        hn = _rms(x_ref[rows, :], nw_ref[...]).astype(BF16)

        def mm(lo, hi):
            return jnp.dot(hn, wr_ref[:, lo:hi], preferred_element_type=F32)

        def mm_t(lo, hi):
            return lax.dot_general(wt_ref[lo:hi, :], hn, (((1,), (1,)), ((), ())), preferred_element_type=F32)

        qq = mm_t(0, DA_WIDTH) * (DA_QKDIM ** -0.5 * LOG2E)
        kk = mm(0, DA_WIDTH)
        vv = mm_t(DA_WIDTH, 2 * DA_WIDTH)
        for h in range(DA_HEADS):
            vt_ref[sub, h * V_ROWS:h * V_ROWS + DA_VDIM, :] = vv[h * DA_VDIM:(h + 1) * DA_VDIM, :].astype(BF16)
            vt_ref[sub, h * V_ROWS + DA_VDIM:(h + 1) * V_ROWS, :] = one_row
        for g in range(N_GROUPS):
            h, c = divmod(g, 2)
            src = h * LANES + c * DA_QKDIM
            dst = g * LANES + c * DA_QKDIM
            spare = g * LANES + (1 - c) * DA_QKDIM
            qt_ref[sub, dst:dst + DA_QKDIM, :] = qq[src:src + DA_QKDIM, :].astype(BF16)
            qt_ref[sub, spare:spare + DA_QKDIM, :] = augq_ref[spare:spare + DA_QKDIM, :].astype(BF16)
            is_data = (lane < DA_QKDIM) if c == 0 else (lane >= DA_QKDIM)
            ka_ref[rows, g * LANES:(g + 1) * LANES] = jnp.where(
                is_data, kk[:, h * LANES:(h + 1) * LANES], augk_ref[:, g * LANES:(g + 1) * LANES]).astype(BF16)

        c = DA_WIDTH
        xh = xh_ref[...] if sub == 0 else x_ref[sub * tm - HALO:sub * tm, :]
        hh = _rms(xh, nw_ref[...]).astype(BF16)
        first = ((i * SUB_TILES + sub) % tiles_per_seq == 0)
        for half in range(2):
            lo = c + half * ML_WIDTH
            hist = jnp.dot(hh, wr_ref[:, lo:lo + ML_WIDTH], preferred_element_type=F32)
            hist = jnp.where(first, 0.0, hist)
            pre = jnp.concatenate([hist, mm(lo, lo + ML_WIDTH)], axis=0)
            y = cb_ref[:, half * ML_WIDTH:(half + 1) * ML_WIDTH]
            for t in range(ML_CONV):
                off = HALO - (ML_CONV - 1) + t
                y = y + cw_ref[t:t + 1, half * ML_WIDTH:(half + 1) * ML_WIDTH] * pre[off:off + tm, :]
            y = y * _sigmoid(y)
            if half == 1:
                y = y * (ML_DIM ** -0.5)
            mqk_ref[rows, half * ML_WIDTH:(half + 1) * ML_WIDTH] = y.astype(BF16)
        c += 2 * ML_WIDTH
        mv_ref[rows, :] = mm(c, c + 512).astype(BF16); c += 512
        mo_ref[rows, :] = mm(c, c + 512).astype(BF16); c += 512
        g_ref[rows, :] = mm(c, c + LANES)[:, :N_GATES]


def _inproj(x2, nw, w_row, w_t, augk, augq_t, cw, cb, S):
    T = x2.shape[0]
    tm = ATT_TILE
    nt = T // tm
    rb = SUB_TILES * tm
    row = lambda i: (i, 0)
    row3 = lambda i: (i, 0, 0)
    halo = lambda i: (jnp.maximum(i * (rb // HALO) - 1, 0), 0)
    const = lambda i: (0, 0)
    resident = lambda shape: pl.BlockSpec(shape, const, pipeline_mode=pl.Buffered(1))
    bf = lambda n: jax.ShapeDtypeStruct((T, n), BF16)
    ga = N_GROUPS * LANES
    return pl.pallas_call(
        functools.partial(_inproj_kernel, tiles_per_seq=S // tm),
        grid=(nt // SUB_TILES,),
        in_specs=[pl.BlockSpec((rb, D_MODEL), row),
                  pl.BlockSpec((HALO, D_MODEL), halo),
                  resident((1, D_MODEL)),
                  resident(w_row.shape),
                  resident(w_t.shape),
                  resident((tm, ga)),
                  resident((ga, tm)),
                  resident((ML_CONV, 2 * ML_WIDTH)),
                  resident((1, 2 * ML_WIDTH))],
        out_specs=[pl.BlockSpec((SUB_TILES, ga, tm), row3), pl.BlockSpec((rb, ga), row),
                   pl.BlockSpec((SUB_TILES, DA_HEADS * V_ROWS, tm), row3), pl.BlockSpec((rb, 1024), row),
                   pl.BlockSpec((rb, 512), row), pl.BlockSpec((rb, 512), row),
                   pl.BlockSpec((rb, N_GATES), row)],
        out_shape=[jax.ShapeDtypeStruct((nt, ga, tm), BF16), bf(ga),
                   jax.ShapeDtypeStruct((nt, DA_HEADS * V_ROWS, tm), BF16), bf(1024), bf(512), bf(512),
                   jax.ShapeDtypeStruct((T, N_GATES), F32)],
        compiler_params=_params("parallel"),
        name="inproj",
    )(x2, x2, nw, w_row, w_t, augk, augq_t, cw, cb)


def _attn_kernel(qt_ref, qn_ref, ka_ref, vt_ref, lam_ref, nw_ref, o_ref, s_sc, mt_sc, acc_sc, *, nq):
    t = ATT_TILE
    qi = pl.program_id(1)
    lv = lam_ref[...]
    lam = (jnp.exp(jnp.sum(lv[0:1] * lv[1:2], axis=-1, keepdims=True))
           - jnp.exp(jnp.sum(lv[2:3] * lv[3:4], axis=-1, keepdims=True)) + LAM_INIT)
    keep = lax.broadcasted_iota(jnp.int32, (t, t), 0) <= lax.broadcasted_iota(jnp.int32, (t, t), 1)

    def scores(h, ki, masked, q_ref=qt_ref):
        k0 = pl.multiple_of(ki * t, t)
        mts = []
        for c in range(2):
            g = 2 * h + c
            s = jnp.dot(ka_ref[pl.ds(k0, t), g * LANES:(g + 1) * LANES], q_ref[0, g * LANES:(g + 1) * LANES, :],
                        preferred_element_type=F32)
            if masked:
                s = jnp.where(keep, s, NEG_BIG)
            s_sc[g] = s
            mts.append(jnp.max(s, axis=0, keepdims=True))
        return mts

    def accumulate(h, ki, mts, stats):
        vt = vt_ref[ki, h * V_ROWS:(h + 1) * V_ROWS, :]
        shift = (ALIBI_SLOPES_LOG2[h] * t) * (ki - qi).astype(F32)
        out = []
        for c in range(2):
            g = 2 * h + c
            m_old = stats[c]
            m_new = jnp.maximum(m_old, mts[c] + shift)
            p = jnp.exp2(s_sc[g] - (m_new - shift))
            acc_sc[g] = jnp.exp2(m_old - m_new) * acc_sc[g] + jnp.dot(vt, p.astype(BF16), preferred_element_type=F32)
            out.append(m_new)
        return out

    def trip(ki, carry):
        prev = carry[0]
        out = [ki]
        for h in range(DA_HEADS):
            ch = carry[1 + 4 * h:5 + 4 * h]
            stats = accumulate(h, prev, ch[:2], ch[2:])
            out += scores(h, ki, False) + stats
        return tuple(out)

    @pl.when(qi == 0)
    def _():
        for h in range(DA_HEADS):
            mt_sc[2 * h], mt_sc[2 * h + 1] = scores(h, qi, True)

    acc_sc[...] = jnp.zeros(acc_sc.shape, F32)
    neg = jnp.full((1, t), NEG_BIG, F32)
    init = [qi]
    for h in range(DA_HEADS):
        init += [mt_sc[2 * h], mt_sc[2 * h + 1], neg, neg]
    carry = lax.fori_loop(0, qi // 2, lambda kp, cr: trip(2 * kp + 1, trip(2 * kp, cr)), tuple(init))
    carry = lax.cond(qi % 2 == 1, lambda cr: trip(qi - 1, cr), lambda cr: cr, carry)

    def last(with_next):
        for h in range(DA_HEADS):
            ch = carry[1 + 4 * h:5 + 4 * h]
            accumulate(h, carry[0], ch[:2], ch[2:])
            if with_next:
                mt_sc[2 * h], mt_sc[2 * h + 1] = scores(h, qi + 1, True, qn_ref)
            a0, a1 = acc_sc[2 * h], acc_sc[2 * h + 1]
            r0 = 1.0 / a0[DA_VDIM:DA_VDIM + 1]
            r1 = lam / a1[DA_VDIM:DA_VDIM + 1]
            o = a0[:DA_VDIM] * r0 - a1[:DA_VDIM] * r1
            o = o * lax.rsqrt(jnp.mean(o * o, axis=0, keepdims=True) + NORM_EPS) * nw_ref[...] * (1.0 - LAM_INIT)
            o_ref[:, h * DA_VDIM:(h + 1) * DA_VDIM] = o.T.astype(BF16)

    pl.when(qi < nq - 1)(lambda: last(True))
    pl.when(qi == nq - 1)(lambda: last(False))


def _attn(qt, ka, vt, lam_vecs, nw_col, B, S):
    t = ATT_TILE
    nq = S // t
    ga = N_GROUPS * LANES
    return pl.pallas_call(
        functools.partial(_attn_kernel, nq=nq),
        grid=(B, nq),
        in_specs=[pl.BlockSpec((1, ga, t), lambda b, i: (b * nq + i, 0, 0)),
                  pl.BlockSpec((1, ga, t), lambda b, i: (b * nq + jnp.minimum(i + 1, nq - 1), 0, 0)),
                  pl.BlockSpec((S, ga), lambda b, i: (b, 0)),
                  pl.BlockSpec((nq, DA_HEADS * V_ROWS, t), lambda b, i: (b, 0, 0)),
                  pl.BlockSpec((4, DA_QKDIM), lambda b, i: (0, 0)),
                  pl.BlockSpec((DA_VDIM, 1), lambda b, i: (0, 0))],
        out_specs=pl.BlockSpec((t, DA_WIDTH), lambda b, i: (b * nq + i, 0)),
        out_shape=jax.ShapeDtypeStruct((B * S, DA_WIDTH), BF16),
        scratch_shapes=[pltpu.VMEM((N_GROUPS, t, t), F32), pltpu.VMEM((N_GROUPS, 1, t), F32),
                        pltpu.VMEM((N_GROUPS, V_ROWS, t), F32)],
        compiler_params=_params("parallel", "arbitrary"),
        name="attn",
    )(qt, qt, ka, vt, lam_vecs, nw_col)


def _log_sigmoid(x):
    return jnp.minimum(x, 0.0) - jnp.log(1.0 + jnp.exp(-jnp.abs(x)))


def _mlstm_kernel(qk_sc, v_ref, o_ref, g_ref, gt_ref, gbr_ref, gbc_ref,
                  nw_ref, out_ref, c_sc, m_sc, cprev_sc, mprev_sc, *, ts):
    j = pl.program_id(1)
    L = ML_CHUNK

    @pl.when(j == 0)
    def _():
        c_sc[...] = jnp.zeros(c_sc.shape, F32)
        m_sc[...] = jnp.zeros(m_sc.shape, F32)

    tt = lax.broadcasted_iota(jnp.int32, (L, L), 0)
    ss = lax.broadcasted_iota(jnp.int32, (L, L), 1)
    causal = ss <= tt
    nc = ts // L
    lane = lax.broadcasted_iota(jnp.int32, (nc, L, ML_DIM), 2)
    ones_col = jnp.where(lane == 0, 1.0, 0.0).astype(BF16)
    g = g_ref[...] + gbr_ref[...]
    gt = gt_ref[...] + gbc_ref[...]
    lf, lft = _log_sigmoid(g), _log_sigmoid(gt)
    bdot = lambda x, y, cx, cy: lax.dot_general(x, y, (((cx,), (cy,)), ((0,), (0,))), preferred_element_type=F32)

    for h in range(ML_HEADS):
        hs = slice(h * ML_DIM, (h + 1) * ML_DIM)
        i_col, f_col = g[:, :, h:h + 1], lf[:, :, ML_HEADS + h:ML_HEADS + h + 1]
        i_row, f_row = gt[:, h:h + 1, :], lft[:, ML_HEADS + h:ML_HEADS + h + 1, :]
        b_col = jnp.sum(jnp.where(causal, f_row, 0.0), axis=2, keepdims=True)
        b_row = jnp.sum(jnp.where(tt <= ss, f_col, 0.0), axis=1, keepdims=True)
        log_d = jnp.where(causal, b_col - b_row + i_row, NEG_BIG)
        a = jnp.max(log_d, axis=2, keepdims=True)
        d_loc = jnp.exp(log_d - a)

        q3 = qk_sc[:, hs].reshape(nc, L, ML_DIM)
        k3 = qk_sc[:, ML_WIDTH + h * ML_DIM:ML_WIDTH + (h + 1) * ML_DIM].reshape(nc, L, ML_DIM)
        v_aug = jnp.concatenate([v_ref[:, hs].reshape(nc, L, ML_DIM), ones_col], axis=2)

        sc = bdot(q3, k3, 2, 2) * d_loc
        intra = bdot(sc.astype(BF16), v_aug, 2, 1)
        b_last = b_col[:, L - 1:L, :]
        a_last = a[:, L - 1:L, :]
        w_loc = jnp.exp(b_last - b_col + i_col - a_last)
        wv = (w_loc * v_aug.astype(F32)).astype(BF16)
        upd = [lax.dot_general(k3[ci], wv[ci], (((0,), (0,)), ((), ())), preferred_element_type=F32)
               for ci in range(nc)]

        c_old = c_sc[h]
        m_old = m_sc[h]
        for ci in range(nc):
            cprev_sc[ci] = c_old.astype(BF16)
            mprev_sc[ci] = m_old
            m_new = jnp.maximum(b_last[ci] + m_old, a_last[ci])
            c_old = jnp.exp(b_last[ci] + m_old - m_new) * c_old + jnp.exp(a_last[ci] - m_new) * upd[ci]
            m_old = m_new
        c_sc[h] = c_old
        m_sc[h] = m_old

        log_prev = b_col + mprev_sc[...]
        m_t = jnp.maximum(log_prev, a)
        inter = bdot(q3, cprev_sc[...], 2, 1)
        tot = jnp.exp(a - m_t) * intra + jnp.exp(log_prev - m_t) * inter
        hh = tot[:, :, :ML_DIM] / jnp.maximum(jnp.abs(tot[:, :, ML_DIM:ML_DIM + 1]), jnp.exp(-m_t))
        hh = _rms(hh, nw_ref[:, hs]).reshape(ts, ML_DIM)
        gate = o_ref[:, hs].astype(F32)
        out_ref[:, hs] = (hh * _sigmoid(gate)).astype(BF16)


def _mlstm(mqk, mv, mo, g3, gt3, gb_row, gb_col, nw, B, S, ts):
    ns = S // ts
    cpb = ts // ML_CHUNK
    row = lambda b, j: (b * ns + j, 0)
    ch = lambda b, j: (b * ns + j, 0, 0)
    const = lambda b, j: (0, 0)
    return pl.pallas_call(
        functools.partial(_mlstm_kernel, ts=ts),
        grid=(B, ns),
        in_specs=[pl.BlockSpec((ts, 2 * ML_WIDTH), row),
                  pl.BlockSpec((ts, ML_WIDTH), row),
                  pl.BlockSpec((ts, ML_WIDTH), row),
                  pl.BlockSpec((cpb, ML_CHUNK, N_GATES), ch),
                  pl.BlockSpec((cpb, N_GATES, ML_CHUNK), ch),
                  pl.BlockSpec((1, N_GATES), const),
                  pl.BlockSpec((N_GATES, 1), const),
                  pl.BlockSpec((1, ML_WIDTH), const)],
        out_specs=pl.BlockSpec((ts, ML_WIDTH), row),
        out_shape=jax.ShapeDtypeStruct((B * S, ML_WIDTH), BF16),
        scratch_shapes=[pltpu.VMEM((ML_HEADS, ML_DIM, 2 * ML_DIM), F32),
                        pltpu.VMEM((ML_HEADS, 1, 1), F32),
                        pltpu.VMEM((cpb, ML_DIM, 2 * ML_DIM), BF16),
                        pltpu.VMEM((cpb, 1, 1), F32)],
        compiler_params=_params("parallel", "arbitrary"),
        name="mlstm",
    )(mqk, mv, mo, g3, gt3, gb_row, gb_col, nw)


FF_CHUNK = 256


def _mixer_kernel(x0_ref, da0_ref, ml0_ref, xn_ref, dan_ref, mln_ref, wa_ref, wm_ref, nw_ref, wup_ref, cw_ref,
                  cb_ref, wd_ref, fw_ref, o_ref, x1_sc, x1n_sc, hn_sc, u_sc, act_sc, *, tm, tiles_per_seq):
    i = pl.program_id(0)

    def prepare(xr, dr, mr):
        x1 = (xr[...] + jnp.dot(dr[...], wa_ref[...], preferred_element_type=F32)
              + jnp.dot(mr[...], wm_ref[...], preferred_element_type=F32))
        x1n_sc[...] = x1
        hn_sc[HALO:, :] = _rms(x1, nw_ref[...]).astype(BF16)

    @pl.when(i == 0)
    def _():
        prepare(x0_ref, da0_ref, ml0_ref)
        hn_sc[0:HALO, :] = jnp.zeros((HALO, D_MODEL), BF16)

    x1_sc[...] = x1n_sc[...]
    nf = D_FF // FF_CHUNK

    def up(j):
        hn = hn_sc[...]
        for part in range(2):
            lo = part * D_FF + j * FF_CHUNK
            u_sc[j % 2, part] = jnp.dot(hn, wup_ref[:, lo:lo + FF_CHUNK], preferred_element_type=F32)

    def activate(j):
        y = []
        for part in range(2):
            lo = part * D_FF + j * FF_CHUNK
            acc = cb_ref[:, lo:lo + FF_CHUNK]
            for t in range(FFN_CONV):
                off = HALO - (FFN_CONV - 1) + t
                acc = acc + cw_ref[t:t + 1, lo:lo + FF_CHUNK] * u_sc[j % 2, part, off:off + tm, :]
            y.append(acc)
        act = y[0] * _sigmoid(y[0]) * y[1]
        act_sc[:, j * FF_CHUNK:(j + 1) * FF_CHUNK] = act.astype(BF16)

    up(0)
    for j in range(nf):
        if j + 1 < nf:
            up(j + 1)
        activate(j)

    hist = _rms(x1_sc[tm - HALO:tm, :], nw_ref[...])
    hn_sc[0:HALO, :] = jnp.where((i + 1) % tiles_per_seq == 0, 0.0, hist).astype(BF16)
    prepare(xn_ref, dan_ref, mln_ref)

    y = x1_sc[...] + jnp.dot(act_sc[...], wd_ref[...], preferred_element_type=F32)
    o_ref[...] = _rms(y, fw_ref[...])


def _mixer(x2, da, ml, wa, wm, nw, w_up, cw, cb, w_down, fw, S, tm):
    T = x2.shape[0]
    n = T // tm
    row = lambda i: (i, 0)
    first = lambda i: (0, 0)
    nxt = lambda i: (jnp.minimum(i + 1, n - 1), 0)
    const = lambda i: (0, 0)
    resident = lambda shape: pl.BlockSpec(shape, const, pipeline_mode=pl.Buffered(1))
    return pl.pallas_call(
        functools.partial(_mixer_kernel, tm=tm, tiles_per_seq=S // tm),
        grid=(n,),
        in_specs=[pl.BlockSpec((tm, D_MODEL), first), pl.BlockSpec((tm, DA_WIDTH), first),
                  pl.BlockSpec((tm, ML_WIDTH), first),
                  pl.BlockSpec((tm, D_MODEL), nxt), pl.BlockSpec((tm, DA_WIDTH), nxt),
                  pl.BlockSpec((tm, ML_WIDTH), nxt),
                  resident((DA_WIDTH, D_MODEL)), resident((ML_WIDTH, D_MODEL)),
                  resident((1, D_MODEL)),
                  resident((D_MODEL, 2 * D_FF)),
                  resident((FFN_CONV, 2 * D_FF)),
                  resident((1, 2 * D_FF)),
                  resident((D_FF, D_MODEL)),
                  resident((1, D_MODEL))],
        out_specs=pl.BlockSpec((tm, D_MODEL), row),
        out_shape=jax.ShapeDtypeStruct((T, D_MODEL), F32),
        scratch_shapes=[pltpu.VMEM((tm, D_MODEL), F32),
                        pltpu.VMEM((tm, D_MODEL), F32),
                        pltpu.VMEM((tm + HALO, D_MODEL), BF16),
                        pltpu.VMEM((2, 2, tm + HALO, FF_CHUNK), F32),
                        pltpu.VMEM((tm, D_FF), BF16)],
        compiler_params=_params("arbitrary"),
        name="mixer",
    )(x2, da, ml, x2, da, ml, wa, wm, nw, w_up, cw, cb, w_down, fw)


def _tile(n, pref):
    t = min(n, pref)
    assert n % t == 0, (n, t)
    return t


def kernel(x, attn_norm_w, w_in, mlstm_conv_w, mlstm_conv_b, mlstm_igate_b, mlstm_fgate_b, lambda_q1, lambda_k1, lambda_q2, lambda_k2, diff_norm_w, mlstm_norm_w, w_out, ffn_norm_w, w_up, ffn_conv_w, ffn_conv_b, w_down, final_norm_w):
    B, S, D = x.shape
    assert D == D_MODEL and S % (SUB_TILES * ATT_TILE) == 0 and attn_norm_w.shape[0] == 1
    T = B * S
    x2 = x.reshape(T, D)

    w = w_in[0]

    w_gate = jnp.pad(w[:, MAIN_COLS:], ((0, 0), (0, LANES - N_GATES)))
    w_row = jnp.concatenate([w[:, 512:1024], w[:, 1536:MAIN_COLS], w_gate], axis=1).astype(BF16)
    w_t = jnp.concatenate([w[:, :512], w[:, 1024:1536]], axis=1).T.astype(BF16)
    augk, augq_t = _score_aug()
    qt, ka, vt, mqk, mv, mo, g = _inproj(x2, attn_norm_w, w_row, w_t, jnp.asarray(augk), jnp.asarray(augq_t),
                                         mlstm_conv_w[0], mlstm_conv_b, S)

    lam_vecs = jnp.concatenate([lambda_q1, lambda_k1, lambda_q2, lambda_k2], axis=0).astype(F32)
    da = _attn(qt, ka, vt, lam_vecs, diff_norm_w.reshape(DA_VDIM, 1), B, S)

    nchunks = T // ML_CHUNK
    g3 = g.reshape(nchunks, ML_CHUNK, N_GATES)
    gt3 = jnp.swapaxes(g3, 1, 2)
    gate_b = jnp.concatenate([mlstm_igate_b, mlstm_fgate_b], axis=1).astype(F32)
    ml = _mlstm(mqk, mv, mo, g3, gt3, gate_b, gate_b.reshape(N_GATES, 1), mlstm_norm_w, B, S, _tile(S, 1024))

    w_o = w_out[0].astype(BF16)
    out = _mixer(x2, da, ml, w_o[:DA_WIDTH], w_o[DA_WIDTH:], ffn_norm_w, w_up[0].astype(BF16), ffn_conv_w[0],
                 ffn_conv_b, w_down[0].astype(BF16), final_norm_w.reshape(1, D), S, _tile(S, 512))
    return out.reshape(B, S, D)
```

```python
import functools
import math

import jax
import jax.numpy as jnp
import numpy as np
from jax import lax
from jax.experimental import pallas as pl
from jax.experimental.pallas import tpu as pltpu

F32 = jnp.float32
BF16 = jnp.bfloat16

D_MODEL = 1024
DA_HEADS = 4
DA_VDIM = 128
DA_QKDIM = 64
DA_WIDTH = DA_HEADS * DA_VDIM
ML_HEADS = 4
ML_DIM = 128
ML_WIDTH = ML_HEADS * ML_DIM
ML_CHUNK = 64
ML_CONV = 4
FFN_CONV = 3
D_FF = 2816
NORM_EPS = 1e-6
LAM_INIT = 0.8 - 0.6 * math.exp(-0.3 * 0)
N_GATES = 2 * ML_HEADS
N_GROUPS = 2 * DA_HEADS
LOG2E = math.log2(math.e)
ALIBI_SLOPES_LOG2 = tuple(float(np.float32(2.0 ** (-8.0 * (h + 1) / DA_HEADS) * LOG2E)) for h in range(DA_HEADS))

LANES = 128
V_ROWS = DA_VDIM + 16
ATT_TILE = 512
HALO = 16
NEG_BIG = -1e30
VMEM_LIMIT = 56 * 1024 * 1024

MAIN_COLS = 2 * DA_WIDTH + DA_WIDTH + 2 * ML_WIDTH + ML_WIDTH + ML_WIDTH


def _params(*sem):
    return pltpu.CompilerParams(dimension_semantics=sem, vmem_limit_bytes=VMEM_LIMIT)


def _sigmoid(x):
    return 0.5 + 0.5 * jnp.tanh(0.5 * x)


def _rms(x, w):
    return x * lax.rsqrt(jnp.mean(x * x, axis=-1, keepdims=True) + NORM_EPS) * w


def _score_aug():
    r = np.arange(ATT_TILE)
    hi = ((r // 16) * 16).astype(np.float32)
    lo = (r % 16).astype(np.float32)
    augk = np.zeros((ATT_TILE, N_GROUPS * LANES), np.float32)
    augq_t = np.zeros((N_GROUPS * LANES, ATT_TILE), np.float32)
    for g in range(N_GROUPS):
        rest = np.float32(ALIBI_SLOPES_LOG2[g // 2])
        parts = []
        for _ in range(3):
            parts.append(np.asarray(rest).astype(BF16).astype(np.float32))
            rest = np.float32(rest - parts[-1])
        assert rest == 0.0
        c0 = g * LANES + (DA_QKDIM if g % 2 == 0 else 0)
        for i, ci in enumerate(parts):
            augk[:, c0 + i] = hi
            augk[:, c0 + 3 + i] = lo
            augk[:, c0 + 6 + i] = -ci
            augk[:, c0 + 9 + i] = -ci
            augq_t[c0 + i] = ci
            augq_t[c0 + 3 + i] = ci
            augq_t[c0 + 6 + i] = hi
            augq_t[c0 + 9 + i] = lo
    return augk, augq_t


SUB_TILES = 2


def _inproj_kernel(x_ref, xh_ref, nw_ref, wr_ref, wt_ref, augk_ref, augq_ref, cw_ref, cb_ref,
                   qt_ref, ka_ref, vt_ref, mqk_ref, mv_ref, mo_ref, g_ref, *, tiles_per_seq):
    i = pl.program_id(0)
    tm = ATT_TILE
    lane = lax.broadcasted_iota(jnp.int32, (tm, LANES), 1)
    one_row = (lax.broadcasted_iota(jnp.int32, (V_ROWS - DA_VDIM, tm), 0) == 0).astype(BF16)

    for sub in range(SUB_TILES):
        rows = slice(sub * tm, (sub + 1) * tm)
        hn = _rms(x_ref[rows, :], nw_ref[...]).astype(BF16)

        def mm(lo, hi):
            return jnp.dot(hn, wr_ref[:, lo:hi], preferred_element_type=F32)

        def mm_t(lo, hi):
            return lax.dot_general(wt_ref[lo:hi, :], hn, (((1,), (1,)), ((), ())), preferred_element_type=F32)

        qq = mm_t(0, DA_WIDTH) * (DA_QKDIM ** -0.5 * LOG2E)
        kk = mm(0, DA_WIDTH)
        vv = mm_t(DA_WIDTH, 2 * DA_WIDTH)
        for h in range(DA_HEADS):
            vt_ref[sub, h * V_ROWS:h * V_ROWS + DA_VDIM, :] = vv[h * DA_VDIM:(h + 1) * DA_VDIM, :].astype(BF16)
            vt_ref[sub, h * V_ROWS + DA_VDIM:(h + 1) * V_ROWS, :] = one_row
        for g in range(N_GROUPS):
            h, c = divmod(g, 2)
            src = h * LANES + c * DA_QKDIM
            dst = g * LANES + c * DA_QKDIM
            spare = g * LANES + (1 - c) * DA_QKDIM
            qt_ref[sub, dst:dst + DA_QKDIM, :] = qq[src:src + DA_QKDIM, :].astype(BF16)
            qt_ref[sub, spare:spare + DA_QKDIM, :] = augq_ref[spare:spare + DA_QKDIM, :].astype(BF16)
            is_data = (lane < DA_QKDIM) if c == 0 else (lane >= DA_QKDIM)
            ka_ref[rows, g * LANES:(g + 1) * LANES] = jnp.where(
                is_data, kk[:, h * LANES:(h + 1) * LANES], augk_ref[:, g * LANES:(g + 1) * LANES]).astype(BF16)

        c = DA_WIDTH
        xh = xh_ref[...] if sub == 0 else x_ref[sub * tm - HALO:sub * tm, :]
        hh = _rms(xh, nw_ref[...]).astype(BF16)
        first = ((i * SUB_TILES + sub) % tiles_per_seq == 0)
        for half in range(2):
            lo = c + half * ML_WIDTH
            hist = jnp.dot(hh, wr_ref[:, lo:lo + ML_WIDTH], preferred_element_type=F32)
            hist = jnp.where(first, 0.0, hist)
            pre = jnp.concatenate([hist, mm(lo, lo + ML_WIDTH)], axis=0)
            y = cb_ref[:, half * ML_WIDTH:(half + 1) * ML_WIDTH]
            for t in range(ML_CONV):
                off = HALO - (ML_CONV - 1) + t
                y = y + cw_ref[t:t + 1, half * ML_WIDTH:(half + 1) * ML_WIDTH] * pre[off:off + tm, :]
            y = y * _sigmoid(y)
            if half == 1:
                y = y * (ML_DIM ** -0.5)
            mqk_ref[rows, half * ML_WIDTH:(half + 1) * ML_WIDTH] = y.astype(BF16)
        c += 2 * ML_WIDTH
        mv_ref[rows, :] = mm(c, c + ML_WIDTH).astype(BF16); c += ML_WIDTH
        mo_ref[rows, :] = mm(c, c + ML_WIDTH).astype(BF16); c += ML_WIDTH
        g_ref[rows, :] = mm(c, c + LANES)[:, :N_GATES]


def _inproj(x2, nw, w_row, w_t, augk, augq_t, cw, cb, S):
    T = x2.shape[0]
    tm = ATT_TILE
    nt = T // tm
    rb = SUB_TILES * tm
    row = lambda i: (i, 0)
    row3 = lambda i: (i, 0, 0)
    halo = lambda i: (jnp.maximum(i * (rb // HALO) - 1, 0), 0)
    const = lambda i: (0, 0)
    resident = lambda shape: pl.BlockSpec(shape, const, pipeline_mode=pl.Buffered(1))
    bf = lambda n: jax.ShapeDtypeStruct((T, n), BF16)
    ga = N_GROUPS * LANES
    return pl.pallas_call(
        functools.partial(_inproj_kernel, tiles_per_seq=S // tm),
        grid=(nt // SUB_TILES,),
        in_specs=[pl.BlockSpec((rb, D_MODEL), row),
                  pl.BlockSpec((HALO, D_MODEL), halo),
                  resident((1, D_MODEL)),
                  resident(w_row.shape),
                  resident(w_t.shape),
                  resident((tm, ga)),
                  resident((ga, tm)),
                  resident((ML_CONV, 2 * ML_WIDTH)),
                  resident((1, 2 * ML_WIDTH))],
        out_specs=[pl.BlockSpec((SUB_TILES, ga, tm), row3), pl.BlockSpec((rb, ga), row),
                   pl.BlockSpec((SUB_TILES, DA_HEADS * V_ROWS, tm), row3), pl.BlockSpec((rb, 2 * ML_WIDTH), row),
                   pl.BlockSpec((rb, ML_WIDTH), row), pl.BlockSpec((rb, ML_WIDTH), row),
                   pl.BlockSpec((rb, N_GATES), row)],
        out_shape=[jax.ShapeDtypeStruct((nt, ga, tm), BF16), bf(ga),
                   jax.ShapeDtypeStruct((nt, DA_HEADS * V_ROWS, tm), BF16), bf(2 * ML_WIDTH), bf(ML_WIDTH), bf(ML_WIDTH),
                   jax.ShapeDtypeStruct((T, N_GATES), F32)],
        compiler_params=_params("parallel"),
        name="inproj",
    )(x2, x2, nw, w_row, w_t, augk, augq_t, cw, cb)


def _attn_kernel(qt_ref, qn_ref, ka_ref, vt_ref, lam_ref, nw_ref, o_ref, s_sc, mt_sc, acc_sc, *, nq):
    t = ATT_TILE
    qi = pl.program_id(1)
    lv = lam_ref[...]
    lam = (jnp.exp(jnp.sum(lv[0:1] * lv[1:2], axis=-1, keepdims=True))
           - jnp.exp(jnp.sum(lv[2:3] * lv[3:4], axis=-1, keepdims=True)) + LAM_INIT)
    keep = lax.broadcasted_iota(jnp.int32, (t, t), 0) <= lax.broadcasted_iota(jnp.int32, (t, t), 1)

    def scores(h, ki, masked, q_ref=qt_ref):
        k0 = pl.multiple_of(ki * t, t)
        mts = []
        for c in range(2):
            g = 2 * h + c
            s = jnp.dot(ka_ref[pl.ds(k0, t), g * LANES:(g + 1) * LANES], q_ref[0, g * LANES:(g + 1) * LANES, :],
                        preferred_element_type=F32)
            if masked:
                s = jnp.where(keep, s, NEG_BIG)
            s_sc[g] = s
            mts.append(jnp.max(s, axis=0, keepdims=True))
        return mts

    def accumulate(h, ki, mts, stats):
        vt = vt_ref[ki, h * V_ROWS:(h + 1) * V_ROWS, :]
        shift = (ALIBI_SLOPES_LOG2[h] * t) * (ki - qi).astype(F32)
        out = []
        for c in range(2):
            g = 2 * h + c
            m_old = stats[c]
            m_new = jnp.maximum(m_old, mts[c] + shift)
            p = jnp.exp2(s_sc[g] - (m_new - shift))
            acc_sc[g] = jnp.exp2(m_old - m_new) * acc_sc[g] + jnp.dot(vt, p.astype(BF16), preferred_element_type=F32)
            out.append(m_new)
        return out

    def trip(ki, carry):
        prev = carry[0]
        out = [ki]
        for h in range(DA_HEADS):
            ch = carry[1 + 4 * h:5 + 4 * h]
            stats = accumulate(h, prev, ch[:2], ch[2:])
            out += scores(h, ki, False) + stats
        return tuple(out)

    @pl.when(qi == 0)
    def _():
        for h in range(DA_HEADS):
            mt_sc[2 * h], mt_sc[2 * h + 1] = scores(h, qi, True)

    acc_sc[...] = jnp.zeros(acc_sc.shape, F32)
    neg = jnp.full((1, t), NEG_BIG, F32)
    init = [qi]
    for h in range(DA_HEADS):
        init += [mt_sc[2 * h], mt_sc[2 * h + 1], neg, neg]
    carry = lax.fori_loop(0, qi // 2, lambda kp, cr: trip(2 * kp + 1, trip(2 * kp, cr)), tuple(init))
    carry = lax.cond(qi % 2 == 1, lambda cr: trip(qi - 1, cr), lambda cr: cr, carry)

    def last(with_next):
        for h in range(DA_HEADS):
            ch = carry[1 + 4 * h:5 + 4 * h]
            accumulate(h, carry[0], ch[:2], ch[2:])
            if with_next:
                mt_sc[2 * h], mt_sc[2 * h + 1] = scores(h, qi + 1, True, qn_ref)
            a0, a1 = acc_sc[2 * h], acc_sc[2 * h + 1]
            r0 = 1.0 / a0[DA_VDIM:DA_VDIM + 1]
            r1 = lam / a1[DA_VDIM:DA_VDIM + 1]
            o = a0[:DA_VDIM] * r0 - a1[:DA_VDIM] * r1
            o = o * lax.rsqrt(jnp.mean(o * o, axis=0, keepdims=True) + NORM_EPS) * nw_ref[...] * (1.0 - LAM_INIT)
            o_ref[:, h * DA_VDIM:(h + 1) * DA_VDIM] = o.T.astype(BF16)

    pl.when(qi < nq - 1)(lambda: last(True))
    pl.when(qi == nq - 1)(lambda: last(False))


def _attn(qt, ka, vt, lam_vecs, nw_col, B, S):
    t = ATT_TILE
    nq = S // t
    ga = N_GROUPS * LANES
    return pl.pallas_call(
        functools.partial(_attn_kernel, nq=nq),
        grid=(B, nq),
        in_specs=[pl.BlockSpec((1, ga, t), lambda b, i: (b * nq + i, 0, 0)),
                  pl.BlockSpec((1, ga, t), lambda b, i: (b * nq + jnp.minimum(i + 1, nq - 1), 0, 0)),
                  pl.BlockSpec((S, ga), lambda b, i: (b, 0)),
                  pl.BlockSpec((nq, DA_HEADS * V_ROWS, t), lambda b, i: (b, 0, 0)),
                  pl.BlockSpec((4, DA_QKDIM), lambda b, i: (0, 0)),
                  pl.BlockSpec((DA_VDIM, 1), lambda b, i: (0, 0))],
        out_specs=pl.BlockSpec((t, DA_WIDTH), lambda b, i: (b * nq + i, 0)),
        out_shape=jax.ShapeDtypeStruct((B * S, DA_WIDTH), BF16),
        scratch_shapes=[pltpu.VMEM((N_GROUPS, t, t), F32), pltpu.VMEM((N_GROUPS, 1, t), F32),
                        pltpu.VMEM((N_GROUPS, V_ROWS, t), F32)],
        compiler_params=_params("parallel", "arbitrary"),
        name="attn",
    )(qt, qt, ka, vt, lam_vecs, nw_col)


ML_STEP = 1024


def _log_sigmoid(x):
    return jnp.minimum(x, 0.0) - jnp.log(1.0 + jnp.exp(-jnp.abs(x)))


def _mlstm_kernel(qk_sc, v_ref, o_ref, g_ref, gt_ref, gbr_ref, gbc_ref,
                  nw_ref, out_ref, c_sc, m_sc, cprev_sc, mprev_sc, *, ts):
    j = pl.program_id(1)
    L = ML_CHUNK

    @pl.when(j == 0)
    def _():
        c_sc[...] = jnp.zeros(c_sc.shape, F32)
        m_sc[...] = jnp.zeros(m_sc.shape, F32)

    tt = lax.broadcasted_iota(jnp.int32, (L, L), 0)
    ss = lax.broadcasted_iota(jnp.int32, (L, L), 1)
    causal = ss <= tt
    nc = ts // L
    lane = lax.broadcasted_iota(jnp.int32, (nc, L, ML_DIM), 2)
    ones_col = jnp.where(lane == 0, 1.0, 0.0).astype(BF16)
    g = g_ref[...] + gbr_ref[...]
    gt = gt_ref[...] + gbc_ref[...]
    lf, lft = _log_sigmoid(g), _log_sigmoid(gt)
    bdot = lambda x, y, cx, cy: lax.dot_general(x, y, (((cx,), (cy,)), ((0,), (0,))), preferred_element_type=F32)

    for h in range(ML_HEADS):
        hs = slice(h * ML_DIM, (h + 1) * ML_DIM)
        i_col, f_col = g[:, :, h:h + 1], lf[:, :, ML_HEADS + h:ML_HEADS + h + 1]
        i_row, f_row = gt[:, h:h + 1, :], lft[:, ML_HEADS + h:ML_HEADS + h + 1, :]
        b_col = jnp.sum(jnp.where(causal, f_row, 0.0), axis=2, keepdims=True)
        b_row = jnp.sum(jnp.where(tt <= ss, f_col, 0.0), axis=1, keepdims=True)
        log_d = jnp.where(causal, b_col - b_row + i_row, NEG_BIG)
        a = jnp.max(log_d, axis=2, keepdims=True)
        d_loc = jnp.exp(log_d - a)

        q3 = qk_sc[:, hs].reshape(nc, L, ML_DIM)
        k3 = qk_sc[:, ML_WIDTH + h * ML_DIM:ML_WIDTH + (h + 1) * ML_DIM].reshape(nc, L, ML_DIM)
        v_aug = jnp.concatenate([v_ref[:, hs].reshape(nc, L, ML_DIM), ones_col], axis=2)

        sc = bdot(q3, k3, 2, 2) * d_loc
        intra = bdot(sc.astype(BF16), v_aug, 2, 1)
        b_last = b_col[:, L - 1:L, :]
        a_last = a[:, L - 1:L, :]
        w_loc = jnp.exp(b_last - b_col + i_col - a_last)
        wv = (w_loc * v_aug.astype(F32)).astype(BF16)
        upd = [lax.dot_general(k3[ci], wv[ci], (((0,), (0,)), ((), ())), preferred_element_type=F32)
               for ci in range(nc)]

        c_old = c_sc[h]
        m_old = m_sc[h]
        for ci in range(nc):
            cprev_sc[ci] = c_old.astype(BF16)
            mprev_sc[ci] = m_old
            m_new = jnp.maximum(b_last[ci] + m_old, a_last[ci])
            c_old = jnp.exp(b_last[ci] + m_old - m_new) * c_old + jnp.exp(a_last[ci] - m_new) * upd[ci]
            m_old = m_new
        c_sc[h] = c_old
        m_sc[h] = m_old

        log_prev = b_col + mprev_sc[...]
        m_t = jnp.maximum(log_prev, a)
        inter = bdot(q3, cprev_sc[...], 2, 1)
        tot = jnp.exp(a - m_t) * intra + jnp.exp(log_prev - m_t) * inter
        hh = tot[:, :, :ML_DIM] / jnp.maximum(jnp.abs(tot[:, :, ML_DIM:ML_DIM + 1]), jnp.exp(-m_t))
        hh = _rms(hh, nw_ref[:, hs]).reshape(ts, ML_DIM)
        gate = o_ref[:, hs].astype(F32)
        out_ref[:, hs] = (hh * _sigmoid(gate)).astype(BF16)


def _mlstm(mqk, mv, mo, g3, gt3, gb_row, gb_col, nw, B, S, ts):
    ns = S // ts
    cpb = ts // ML_CHUNK
    row = lambda b, j: (b * ns + j, 0)
    ch = lambda b, j: (b * ns + j, 0, 0)
    const = lambda b, j: (0, 0)
    return pl.pallas_call(
        functools.partial(_mlstm_kernel, ts=ts),
        grid=(B, ns),
        in_specs=[pl.BlockSpec((ts, 2 * ML_WIDTH), row),
                  pl.BlockSpec((ts, ML_WIDTH), row),
                  pl.BlockSpec((ts, ML_WIDTH), row),
                  pl.BlockSpec((cpb, ML_CHUNK, N_GATES), ch),
                  pl.BlockSpec((cpb, N_GATES, ML_CHUNK), ch),
                  pl.BlockSpec((1, N_GATES), const),
                  pl.BlockSpec((N_GATES, 1), const),
                  pl.BlockSpec((1, ML_WIDTH), const)],
        out_specs=pl.BlockSpec((ts, ML_WIDTH), row),
        out_shape=jax.ShapeDtypeStruct((B * S, ML_WIDTH), BF16),
        scratch_shapes=[pltpu.VMEM((ML_HEADS, ML_DIM, 2 * ML_DIM), F32),
                        pltpu.VMEM((ML_HEADS, 1, 1), F32),
                        pltpu.VMEM((cpb, ML_DIM, 2 * ML_DIM), BF16),
                        pltpu.VMEM((cpb, 1, 1), F32)],
        compiler_params=_params("parallel", "arbitrary"),
        name="mlstm",
    )(mqk, mv, mo, g3, gt3, gb_row, gb_col, nw)


FF_CHUNK = 256
FF_ROWS = 512


def _mixer_kernel(x0_ref, da0_ref, ml0_ref, xn_ref, dan_ref, mln_ref, wa_ref, wm_ref, nw_ref, wup_ref, cw_ref,
                  cb_ref, wd_ref, fw_ref, o_ref, x1_sc, x1n_sc, hn_sc, u_sc, act_sc, *, tm, tiles_per_seq):
    i = pl.program_id(0)

    def prepare(xr, dr, mr):
        x1 = (xr[...] + jnp.dot(dr[...], wa_ref[...], preferred_element_type=F32)
              + jnp.dot(mr[...], wm_ref[...], preferred_element_type=F32))
        x1n_sc[...] = x1
        hn_sc[HALO:, :] = _rms(x1, nw_ref[...]).astype(BF16)

    @pl.when(i == 0)
    def _():
        prepare(x0_ref, da0_ref, ml0_ref)
        hn_sc[0:HALO, :] = jnp.zeros((HALO, D_MODEL), BF16)

    x1_sc[...] = x1n_sc[...]
    nf = D_FF // FF_CHUNK

    def up(j):
        hn = hn_sc[...]
        for part in range(2):
            lo = part * D_FF + j * FF_CHUNK
            u_sc[j % 2, part] = jnp.dot(hn, wup_ref[:, lo:lo + FF_CHUNK], preferred_element_type=F32)

    def activate(j):
        y = []
        for part in range(2):
            lo = part * D_FF + j * FF_CHUNK
            acc = cb_ref[:, lo:lo + FF_CHUNK]
            for t in range(FFN_CONV):
                off = HALO - (FFN_CONV - 1) + t
                acc = acc + cw_ref[t:t + 1, lo:lo + FF_CHUNK] * u_sc[j % 2, part, off:off + tm, :]
            y.append(acc)
        act = y[0] * _sigmoid(y[0]) * y[1]
        act_sc[:, j * FF_CHUNK:(j + 1) * FF_CHUNK] = act.astype(BF16)

    up(0)
    for j in range(nf):
        if j + 1 < nf:
            up(j + 1)
        activate(j)

    hist = _rms(x1_sc[tm - HALO:tm, :], nw_ref[...])
    hn_sc[0:HALO, :] = jnp.where((i + 1) % tiles_per_seq == 0, 0.0, hist).astype(BF16)
    prepare(xn_ref, dan_ref, mln_ref)

    y = x1_sc[...] + jnp.dot(act_sc[...], wd_ref[...], preferred_element_type=F32)
    o_ref[...] = _rms(y, fw_ref[...])


def _mixer(x2, da, ml, wa, wm, nw, w_up, cw, cb, w_down, fw, S, tm):
    T = x2.shape[0]
    n = T // tm
    row = lambda i: (i, 0)
    nxt = lambda i: (jnp.minimum(i + 1, n - 1), 0)
    const = lambda i: (0, 0)
    resident = lambda shape: pl.BlockSpec(shape, const, pipeline_mode=pl.Buffered(1))
    return pl.pallas_call(
        functools.partial(_mixer_kernel, tm=tm, tiles_per_seq=S // tm),
        grid=(n,),
        in_specs=[resident((tm, D_MODEL)), resident((tm, DA_WIDTH)), resident((tm, ML_WIDTH)),
                  pl.BlockSpec((tm, D_MODEL), nxt), pl.BlockSpec((tm, DA_WIDTH), nxt),
                  pl.BlockSpec((tm, ML_WIDTH), nxt),
                  resident((DA_WIDTH, D_MODEL)), resident((ML_WIDTH, D_MODEL)),
                  resident((1, D_MODEL)),
                  resident((D_MODEL, 2 * D_FF)),
                  resident((FFN_CONV, 2 * D_FF)),
                  resident((1, 2 * D_FF)),
                  resident((D_FF, D_MODEL)),
                  resident((1, D_MODEL))],
        out_specs=pl.BlockSpec((tm, D_MODEL), row),
        out_shape=jax.ShapeDtypeStruct((T, D_MODEL), F32),
        scratch_shapes=[pltpu.VMEM((tm, D_MODEL), F32),
                        pltpu.VMEM((tm, D_MODEL), F32),
                        pltpu.VMEM((tm + HALO, D_MODEL), BF16),
                        pltpu.VMEM((2, 2, tm + HALO, FF_CHUNK), F32),
                        pltpu.VMEM((tm, D_FF), BF16)],
        compiler_params=_params("arbitrary"),
        name="mixer",
    )(x2, da, ml, x2, da, ml, wa, wm, nw, w_up, cw, cb, w_down, fw)


def _tile(n, pref):
    t = min(n, pref)
    assert n % t == 0, (n, t)
    return t


def kernel(x, attn_norm_w, w_in, mlstm_conv_w, mlstm_conv_b, mlstm_igate_b, mlstm_fgate_b, lambda_q1, lambda_k1, lambda_q2, lambda_k2, diff_norm_w, mlstm_norm_w, w_out, ffn_norm_w, w_up, ffn_conv_w, ffn_conv_b, w_down, final_norm_w):
    B, S, D = x.shape
    assert D == D_MODEL and S % (SUB_TILES * ATT_TILE) == 0 and attn_norm_w.shape[0] == 1
    T = B * S
    x2 = x.reshape(T, D)

    w = w_in[0]

    w_gate = jnp.pad(w[:, MAIN_COLS:], ((0, 0), (0, LANES - N_GATES)))
    q0, k0, v0, m0 = 0, DA_WIDTH, 2 * DA_WIDTH, 3 * DA_WIDTH
    w_row = jnp.concatenate([w[:, k0:v0], w[:, m0:MAIN_COLS], w_gate], axis=1).astype(BF16)
    w_t = jnp.concatenate([w[:, q0:k0], w[:, v0:m0]], axis=1).T.astype(BF16)
    augk, augq_t = _score_aug()
    qt, ka, vt, mqk, mv, mo, g = _inproj(x2, attn_norm_w, w_row, w_t, jnp.asarray(augk), jnp.asarray(augq_t),
                                         mlstm_conv_w[0], mlstm_conv_b, S)

    lam_vecs = jnp.concatenate([lambda_q1, lambda_k1, lambda_q2, lambda_k2], axis=0).astype(F32)
    da = _attn(qt, ka, vt, lam_vecs, diff_norm_w.reshape(DA_VDIM, 1), B, S)

    nchunks = T // ML_CHUNK
    g3 = g.reshape(nchunks, ML_CHUNK, N_GATES)
    gt3 = jnp.swapaxes(g3, 1, 2)
    gate_b = jnp.concatenate([mlstm_igate_b, mlstm_fgate_b], axis=1).astype(F32)
    ml = _mlstm(mqk, mv, mo, g3, gt3, gate_b, gate_b.reshape(N_GATES, 1), mlstm_norm_w, B, S, _tile(S, ML_STEP))

    w_o = w_out[0].astype(BF16)
    out = _mixer(x2, da, ml, w_o[:DA_WIDTH], w_o[DA_WIDTH:], ffn_norm_w, w_up[0].astype(BF16), ffn_conv_w[0],
                 ffn_conv_b, w_down[0].astype(BF16), final_norm_w.reshape(1, D), S, _tile(S, FF_ROWS))
    return out.reshape(B, S, D)
```

```python
import functools
import math

import jax
import jax.numpy as jnp
import numpy as np
from jax import lax
from jax.experimental import pallas as pl
from jax.experimental.pallas import tpu as pltpu

F32 = jnp.float32
BF16 = jnp.bfloat16

D_MODEL = 1024
DA_HEADS = 4
DA_VDIM = 128
DA_QKDIM = 64
DA_WIDTH = DA_HEADS * DA_VDIM
ML_HEADS = 4
ML_DIM = 128
ML_WIDTH = ML_HEADS * ML_DIM
ML_CHUNK = 64
ML_CONV = 4
FFN_CONV = 3
D_FF = 2816
NORM_EPS = 1e-6
LAM_INIT = 0.8 - 0.6 * math.exp(-0.3 * 0)
N_GATES = 2 * ML_HEADS
N_GROUPS = 2 * DA_HEADS
LOG2E = math.log2(math.e)
ALIBI_SLOPES_LOG2 = tuple(float(np.float32(2.0 ** (-8.0 * (h + 1) / DA_HEADS) * LOG2E)) for h in range(DA_HEADS))

LANES = 128
V_ROWS = DA_VDIM + 16
ATT_TILE = 512
HALO = 16
NEG_BIG = -1e30
VMEM_LIMIT = 56 * 1024 * 1024

MAIN_COLS = 2 * DA_WIDTH + DA_WIDTH + 2 * ML_WIDTH + ML_WIDTH + ML_WIDTH


def _params(*sem):
    return pltpu.CompilerParams(dimension_semantics=sem, vmem_limit_bytes=VMEM_LIMIT)


def _sigmoid(x):
    return 0.5 + 0.5 * jnp.tanh(0.5 * x)


def _rms(x, w):
    return x * lax.rsqrt(jnp.mean(x * x, axis=-1, keepdims=True) + NORM_EPS) * w


def _score_aug():
    r = np.arange(ATT_TILE)
    hi = ((r // 16) * 16).astype(np.float32)
    lo = (r % 16).astype(np.float32)
    augk = np.zeros((ATT_TILE, N_GROUPS * LANES), np.float32)
    augq_t = np.zeros((N_GROUPS * LANES, ATT_TILE), np.float32)
    for g in range(N_GROUPS):
        rest = np.float32(ALIBI_SLOPES_LOG2[g // 2])
        parts = []
        for _ in range(3):
            parts.append(np.asarray(rest).astype(BF16).astype(np.float32))
            rest = np.float32(rest - parts[-1])
        assert rest == 0.0
        c0 = g * LANES + (DA_QKDIM if g % 2 == 0 else 0)
        for i, ci in enumerate(parts):
            augk[:, c0 + i] = hi
            augk[:, c0 + 3 + i] = lo
            augk[:, c0 + 6 + i] = -ci
            augk[:, c0 + 9 + i] = -ci
            augq_t[c0 + i] = ci
            augq_t[c0 + 3 + i] = ci
            augq_t[c0 + 6 + i] = hi
            augq_t[c0 + 9 + i] = lo
    return augk, augq_t


SUB_TILES = 2


def _inproj_kernel(x_ref, xh_ref, nw_ref, wr_ref, wt_ref, augk_ref, augq_ref, cw_ref, cb_ref,
                   qt_ref, ka_ref, vt_ref, mqk_ref, mv_ref, mo_ref, g_ref, *, tiles_per_seq):
    i = pl.program_id(0)
    tm = ATT_TILE
    lane = lax.broadcasted_iota(jnp.int32, (tm, LANES), 1)
    one_row = (lax.broadcasted_iota(jnp.int32, (V_ROWS - DA_VDIM, tm), 0) == 0).astype(BF16)

    for sub in range(SUB_TILES):
        rows = slice(sub * tm, (sub + 1) * tm)
        hn = _rms(x_ref[rows, :], nw_ref[...]).astype(BF16)

        def mm(lo, hi):
            return jnp.dot(hn, wr_ref[:, lo:hi], preferred_element_type=F32)

        def mm_t(lo, hi):
            return lax.dot_general(wt_ref[lo:hi, :], hn, (((1,), (1,)), ((), ())), preferred_element_type=F32)

        qq = mm_t(0, DA_WIDTH) * (DA_QKDIM ** -0.5 * LOG2E)
        kk = mm(0, DA_WIDTH)
        vv = mm_t(DA_WIDTH, 2 * DA_WIDTH)
        for h in range(DA_HEADS):
            vt_ref[sub, h * V_ROWS:h * V_ROWS + DA_VDIM, :] = vv[h * DA_VDIM:(h + 1) * DA_VDIM, :].astype(BF16)
            vt_ref[sub, h * V_ROWS + DA_VDIM:(h + 1) * V_ROWS, :] = one_row
        for g in range(N_GROUPS):
            h, c = divmod(g, 2)
            src = h * LANES + c * DA_QKDIM
            dst = g * LANES + c * DA_QKDIM
            spare = g * LANES + (1 - c) * DA_QKDIM
            qt_ref[sub, dst:dst + DA_QKDIM, :] = qq[src:src + DA_QKDIM, :].astype(BF16)
            qt_ref[sub, spare:spare + DA_QKDIM, :] = augq_ref[spare:spare + DA_QKDIM, :].astype(BF16)
            is_data = (lane < DA_QKDIM) if c == 0 else (lane >= DA_QKDIM)
            ka_ref[rows, g * LANES:(g + 1) * LANES] = jnp.where(
                is_data, kk[:, h * LANES:(h + 1) * LANES], augk_ref[:, g * LANES:(g + 1) * LANES]).astype(BF16)

        c = DA_WIDTH
        xh = xh_ref[...] if sub == 0 else x_ref[sub * tm - HALO:sub * tm, :]
        hh = _rms(xh, nw_ref[...]).astype(BF16)
        first = ((i * SUB_TILES + sub) % tiles_per_seq == 0)
        for half in range(2):
            lo = c + half * ML_WIDTH
            hist = jnp.dot(hh, wr_ref[:, lo:lo + ML_WIDTH], preferred_element_type=F32)
            hist = jnp.where(first, 0.0, hist)
            pre = jnp.concatenate([hist, mm(lo, lo + ML_WIDTH)], axis=0)
            y = cb_ref[:, half * ML_WIDTH:(half + 1) * ML_WIDTH]
            for t in range(ML_CONV):
                off = HALO - (ML_CONV - 1) + t
                y = y + cw_ref[t:t + 1, half * ML_WIDTH:(half + 1) * ML_WIDTH] * pre[off:off + tm, :]
            y = y * _sigmoid(y)
            if half == 1:
                y = y * (ML_DIM ** -0.5)
            mqk_ref[rows, half * ML_WIDTH:(half + 1) * ML_WIDTH] = y.astype(BF16)
        c += 2 * ML_WIDTH
        mv_ref[rows, :] = mm(c, c + ML_WIDTH).astype(BF16); c += ML_WIDTH
        mo_ref[rows, :] = mm(c, c + ML_WIDTH).astype(BF16); c += ML_WIDTH
        g_ref[rows, :] = mm(c, c + LANES)[:, :N_GATES]


def _inproj(x2, nw, w_row, w_t, augk, augq_t, cw, cb, S):
    T = x2.shape[0]
    tm = ATT_TILE
    nt = T // tm
    rb = SUB_TILES * tm
    row = lambda i: (i, 0)
    row3 = lambda i: (i, 0, 0)
    halo = lambda i: (jnp.maximum(i * (rb // HALO) - 1, 0), 0)
    const = lambda i: (0, 0)
    resident = lambda shape: pl.BlockSpec(shape, const, pipeline_mode=pl.Buffered(1))
    bf = lambda n: jax.ShapeDtypeStruct((T, n), BF16)
    ga = N_GROUPS * LANES
    return pl.pallas_call(
        functools.partial(_inproj_kernel, tiles_per_seq=S // tm),
        grid=(nt // SUB_TILES,),
        in_specs=[pl.BlockSpec((rb, D_MODEL), row),
                  pl.BlockSpec((HALO, D_MODEL), halo),
                  resident((1, D_MODEL)),
                  resident(w_row.shape),
                  resident(w_t.shape),
                  resident((tm, ga)),
                  resident((ga, tm)),
                  resident((ML_CONV, 2 * ML_WIDTH)),
                  resident((1, 2 * ML_WIDTH))],
        out_specs=[pl.BlockSpec((SUB_TILES, ga, tm), row3), pl.BlockSpec((rb, ga), row),
                   pl.BlockSpec((SUB_TILES, DA_HEADS * V_ROWS, tm), row3), pl.BlockSpec((rb, 2 * ML_WIDTH), row),
                   pl.BlockSpec((rb, ML_WIDTH), row), pl.BlockSpec((rb, ML_WIDTH), row),
                   pl.BlockSpec((rb, N_GATES), row)],
        out_shape=[jax.ShapeDtypeStruct((nt, ga, tm), BF16), bf(ga),
                   jax.ShapeDtypeStruct((nt, DA_HEADS * V_ROWS, tm), BF16), bf(2 * ML_WIDTH), bf(ML_WIDTH), bf(ML_WIDTH),
                   jax.ShapeDtypeStruct((T, N_GATES), F32)],
        compiler_params=_params("parallel"),
        name="inproj",
    )(x2, x2, nw, w_row, w_t, augk, augq_t, cw, cb)


def _attn_kernel(qt_ref, qn_ref, ka_ref, vt_ref, lam_ref, nw_ref, o_ref, s_sc, mt_sc, acc_sc, *, nq):
    t = ATT_TILE
    qi = pl.program_id(1)
    lv = lam_ref[...]
    lam = (jnp.exp(jnp.sum(lv[0:1] * lv[1:2], axis=-1, keepdims=True))
           - jnp.exp(jnp.sum(lv[2:3] * lv[3:4], axis=-1, keepdims=True)) + LAM_INIT)
    keep = lax.broadcasted_iota(jnp.int32, (t, t), 0) <= lax.broadcasted_iota(jnp.int32, (t, t), 1)

    def scores(h, ki, masked, q_ref=qt_ref):
        k0 = pl.multiple_of(ki * t, t)
        mts = []
        for c in range(2):
            g = 2 * h + c
            s = jnp.dot(ka_ref[pl.ds(k0, t), g * LANES:(g + 1) * LANES], q_ref[0, g * LANES:(g + 1) * LANES, :],
                        preferred_element_type=F32)
            if masked:
                s = jnp.where(keep, s, NEG_BIG)
            s_sc[g] = s
            mts.append(jnp.max(s, axis=0, keepdims=True))
        return mts

    def accumulate(h, ki, mts, stats):
        vt = vt_ref[ki, h * V_ROWS:(h + 1) * V_ROWS, :]
        shift = (ALIBI_SLOPES_LOG2[h] * t) * (ki - qi).astype(F32)
        out = []
        for c in range(2):
            g = 2 * h + c
            m_old = stats[c]
            m_new = jnp.maximum(m_old, mts[c] + shift)
            p = jnp.exp2(s_sc[g] - (m_new - shift))
            acc_sc[g] = jnp.exp2(m_old - m_new) * acc_sc[g] + jnp.dot(vt, p.astype(BF16), preferred_element_type=F32)
            out.append(m_new)
        return out

    def trip(ki, carry):
        prev = carry[0]
        out = [ki]
        for h in range(DA_HEADS):
            ch = carry[1 + 4 * h:5 + 4 * h]
            stats = accumulate(h, prev, ch[:2], ch[2:])
            out += scores(h, ki, False) + stats
        return tuple(out)

    @pl.when(qi == 0)
    def _():
        for h in range(DA_HEADS):
            mt_sc[2 * h], mt_sc[2 * h + 1] = scores(h, qi, True)

    acc_sc[...] = jnp.zeros(acc_sc.shape, F32)
    neg = jnp.full((1, t), NEG_BIG, F32)
    init = [qi]
    for h in range(DA_HEADS):
        init += [mt_sc[2 * h], mt_sc[2 * h + 1], neg, neg]
    carry = lax.fori_loop(0, qi // 2, lambda kp, cr: trip(2 * kp + 1, trip(2 * kp, cr)), tuple(init))
    carry = lax.cond(qi % 2 == 1, lambda cr: trip(qi - 1, cr), lambda cr: cr, carry)

    def last(with_next):
        for h in range(DA_HEADS):
            ch = carry[1 + 4 * h:5 + 4 * h]
            accumulate(h, carry[0], ch[:2], ch[2:])
            if with_next:
                mt_sc[2 * h], mt_sc[2 * h + 1] = scores(h, qi + 1, True, qn_ref)
            a0, a1 = acc_sc[2 * h], acc_sc[2 * h + 1]
            r0 = 1.0 / a0[DA_VDIM:DA_VDIM + 1]
            r1 = lam / a1[DA_VDIM:DA_VDIM + 1]
            o = a0[:DA_VDIM] * r0 - a1[:DA_VDIM] * r1
            o = o * lax.rsqrt(jnp.mean(o * o, axis=0, keepdims=True) + NORM_EPS) * nw_ref[...] * (1.0 - LAM_INIT)
            o_ref[:, h * DA_VDIM:(h + 1) * DA_VDIM] = o.T.astype(BF16)

    pl.when(qi < nq - 1)(lambda: last(True))
    pl.when(qi == nq - 1)(lambda: last(False))


def _attn(qt, ka, vt, lam_vecs, nw_col, B, S):
    t = ATT_TILE
    nq = S // t
    ga = N_GROUPS * LANES
    return pl.pallas_call(
        functools.partial(_attn_kernel, nq=nq),
        grid=(B, nq),
        in_specs=[pl.BlockSpec((1, ga, t), lambda b, i: (b * nq + i, 0, 0)),
                  pl.BlockSpec((1, ga, t), lambda b, i: (b * nq + jnp.minimum(i + 1, nq - 1), 0, 0)),
                  pl.BlockSpec((S, ga), lambda b, i: (b, 0)),
                  pl.BlockSpec((nq, DA_HEADS * V_ROWS, t), lambda b, i: (b, 0, 0)),
                  pl.BlockSpec((4, DA_QKDIM), lambda b, i: (0, 0)),
                  pl.BlockSpec((DA_VDIM, 1), lambda b, i: (0, 0))],
        out_specs=pl.BlockSpec((t, DA_WIDTH), lambda b, i: (b * nq + i, 0)),
        out_shape=jax.ShapeDtypeStruct((B * S, DA_WIDTH), BF16),
        scratch_shapes=[pltpu.VMEM((N_GROUPS, t, t), F32), pltpu.VMEM((N_GROUPS, 1, t), F32),
                        pltpu.VMEM((N_GROUPS, V_ROWS, t), F32)],
        compiler_params=_params("parallel", "arbitrary"),
        name="attn",
    )(qt, qt, ka, vt, lam_vecs, nw_col)


ML_STEP = 1024


def _log_sigmoid(x):
    return jnp.minimum(x, 0.0) - jnp.log(1.0 + jnp.exp(-jnp.abs(x)))


def _mlstm_kernel(qk_sc, v_ref, o_ref, g_ref, gt_ref, gbr_ref, gbc_ref,
                  nw_ref, out_ref, c_sc, m_sc, cprev_sc, mprev_sc, *, ts):
    j = pl.program_id(1)
    L = ML_CHUNK

    @pl.when(j == 0)
    def _():
        c_sc[...] = jnp.zeros(c_sc.shape, F32)
        m_sc[...] = jnp.zeros(m_sc.shape, F32)

    tt = lax.broadcasted_iota(jnp.int32, (L, L), 0)
    ss = lax.broadcasted_iota(jnp.int32, (L, L), 1)
    causal = ss <= tt
    nc = ts // L
    lane = lax.broadcasted_iota(jnp.int32, (nc, L, ML_DIM), 2)
    ones_col = jnp.where(lane == 0, 1.0, 0.0).astype(BF16)
    g = g_ref[...] + gbr_ref[...]
    gt = gt_ref[...] + gbc_ref[...]
    lf, lft = _log_sigmoid(g), _log_sigmoid(gt)
    bdot = lambda x, y, cx, cy: lax.dot_general(x, y, (((cx,), (cy,)), ((0,), (0,))), preferred_element_type=F32)

    for h in range(ML_HEADS):
        hs = slice(h * ML_DIM, (h + 1) * ML_DIM)
        i_col, f_col = g[:, :, h:h + 1], lf[:, :, ML_HEADS + h:ML_HEADS + h + 1]
        i_row, f_row = gt[:, h:h + 1, :], lft[:, ML_HEADS + h:ML_HEADS + h + 1, :]
        b_col = jnp.sum(jnp.where(causal, f_row, 0.0), axis=2, keepdims=True)
        b_row = jnp.sum(jnp.where(tt <= ss, f_col, 0.0), axis=1, keepdims=True)
        log_d = jnp.where(causal, b_col - b_row + i_row, NEG_BIG)
        a = jnp.max(log_d, axis=2, keepdims=True)
        d_loc = jnp.exp(log_d - a)

        q3 = qk_sc[:, hs].reshape(nc, L, ML_DIM)
        k3 = qk_sc[:, ML_WIDTH + h * ML_DIM:ML_WIDTH + (h + 1) * ML_DIM].reshape(nc, L, ML_DIM)
        v_aug = jnp.concatenate([v_ref[:, hs].reshape(nc, L, ML_DIM), ones_col], axis=2)

        sc = bdot(q3, k3, 2, 2) * d_loc
        intra = bdot(sc.astype(BF16), v_aug, 2, 1)
        b_last = b_col[:, L - 1:L, :]
        a_last = a[:, L - 1:L, :]
        w_loc = jnp.exp(b_last - b_col + i_col - a_last)
        wv = (w_loc * v_aug.astype(F32)).astype(BF16)
        upd = [lax.dot_general(k3[ci], wv[ci], (((0,), (0,)), ((), ())), preferred_element_type=F32)
               for ci in range(nc)]

        c_old = c_sc[h]
        m_old = m_sc[h]
        for ci in range(nc):
            cprev_sc[ci] = c_old.astype(BF16)
            mprev_sc[ci] = m_old
            m_new = jnp.maximum(b_last[ci] + m_old, a_last[ci])
            c_old = jnp.exp(b_last[ci] + m_old - m_new) * c_old + jnp.exp(a_last[ci] - m_new) * upd[ci]
            m_old = m_new
        c_sc[h] = c_old
        m_sc[h] = m_old

        log_prev = b_col + mprev_sc[...]
        m_t = jnp.maximum(log_prev, a)
        inter = bdot(q3, cprev_sc[...], 2, 1)
        tot = jnp.exp(a - m_t) * intra + jnp.exp(log_prev - m_t) * inter
        hh = tot[:, :, :ML_DIM] / jnp.maximum(jnp.abs(tot[:, :, ML_DIM:ML_DIM + 1]), jnp.exp(-m_t))
        hh = _rms(hh, nw_ref[:, hs]).reshape(ts, ML_DIM)
        gate = o_ref[:, hs].astype(F32)
        out_ref[:, hs] = (hh * _sigmoid(gate)).astype(BF16)


def _mlstm(mqk, mv, mo, g3, gt3, gb_row, gb_col, nw, B, S, ts):
    ns = S // ts
    cpb = ts // ML_CHUNK
    row = lambda b, j: (b * ns + j, 0)
    ch = lambda b, j: (b * ns + j, 0, 0)
    const = lambda b, j: (0, 0)
    return pl.pallas_call(
        functools.partial(_mlstm_kernel, ts=ts),
        grid=(B, ns),
        in_specs=[pl.BlockSpec((ts, 2 * ML_WIDTH), row),
                  pl.BlockSpec((ts, ML_WIDTH), row),
                  pl.BlockSpec((ts, ML_WIDTH), row),
                  pl.BlockSpec((cpb, ML_CHUNK, N_GATES), ch),
                  pl.BlockSpec((cpb, N_GATES, ML_CHUNK), ch),
                  pl.BlockSpec((1, N_GATES), const),
                  pl.BlockSpec((N_GATES, 1), const),
                  pl.BlockSpec((1, ML_WIDTH), const)],
        out_specs=pl.BlockSpec((ts, ML_WIDTH), row),
        out_shape=jax.ShapeDtypeStruct((B * S, ML_WIDTH), BF16),
        scratch_shapes=[pltpu.VMEM((ML_HEADS, ML_DIM, 2 * ML_DIM), F32),
                        pltpu.VMEM((ML_HEADS, 1, 1), F32),
                        pltpu.VMEM((cpb, ML_DIM, 2 * ML_DIM), BF16),
                        pltpu.VMEM((cpb, 1, 1), F32)],
        compiler_params=_params("parallel", "arbitrary"),
        name="mlstm",
    )(mqk, mv, mo, g3, gt3, gb_row, gb_col, nw)


FF_CHUNK = 256
FF_ROWS = 512


def _mixer_kernel(x0_ref, da0_ref, ml0_ref, xn_ref, dan_ref, mln_ref, wa_ref, wm_ref, nw_ref, wup_ref, cw_ref,
                  cb_ref, wd_ref, fw_ref, o_ref, x1_sc, x1n_sc, hn_sc, u_sc, act_sc, *, tm, tiles_per_seq):
    i = pl.program_id(0)

    def prepare(xr, dr, mr):
        x1 = (xr[...] + jnp.dot(dr[...], wa_ref[...], preferred_element_type=F32)
              + jnp.dot(mr[...], wm_ref[...], preferred_element_type=F32))
        x1n_sc[...] = x1
        hn_sc[HALO:, :] = _rms(x1, nw_ref[...]).astype(BF16)

    @pl.when(i == 0)
    def _():
        prepare(x0_ref, da0_ref, ml0_ref)
        hn_sc[0:HALO, :] = jnp.zeros((HALO, D_MODEL), BF16)

    x1_sc[...] = x1n_sc[...]
    nf = D_FF // FF_CHUNK

    hr = tm // 2

    def up(j, r):
        hn = hn_sc[r * hr:r * hr + hr + HALO, :]
        for part in range(2):
            lo = part * D_FF + j * FF_CHUNK
            u_sc[j % 2, part, r * (hr + HALO):(r + 1) * (hr + HALO)] = jnp.dot(
                hn, wup_ref[:, lo:lo + FF_CHUNK], preferred_element_type=F32)

    def activate(j, r):
        y = []
        base = r * (hr + HALO)
        for part in range(2):
            lo = part * D_FF + j * FF_CHUNK
            acc = cb_ref[:, lo:lo + FF_CHUNK]
            for t in range(FFN_CONV):
                off = base + HALO - (FFN_CONV - 1) + t
                acc = acc + cw_ref[t:t + 1, lo:lo + FF_CHUNK] * u_sc[j % 2, part, off:off + hr, :]
            y.append(acc)
        act = y[0] * _sigmoid(y[0]) * y[1]
        act_sc[r * hr:(r + 1) * hr, j * FF_CHUNK:(j + 1) * FF_CHUNK] = act.astype(BF16)

    up(0, 0)
    up(0, 1)
    for j in range(nf):
        for r in range(2):
            if j + 1 < nf:
                up(j + 1, r)
            activate(j, r)

    hist = _rms(x1_sc[tm - HALO:tm, :], nw_ref[...])
    hn_sc[0:HALO, :] = jnp.where((i + 1) % tiles_per_seq == 0, 0.0, hist).astype(BF16)
    prepare(xn_ref, dan_ref, mln_ref)

    y = x1_sc[...] + jnp.dot(act_sc[...], wd_ref[...], preferred_element_type=F32)
    o_ref[...] = _rms(y, fw_ref[...])


def _mixer(x2, da, ml, wa, wm, nw, w_up, cw, cb, w_down, fw, S, tm):
    T = x2.shape[0]
    n = T // tm
    row = lambda i: (i, 0)
    nxt = lambda i: (jnp.minimum(i + 1, n - 1), 0)
    const = lambda i: (0, 0)
    resident = lambda shape: pl.BlockSpec(shape, const, pipeline_mode=pl.Buffered(1))
    return pl.pallas_call(
        functools.partial(_mixer_kernel, tm=tm, tiles_per_seq=S // tm),
        grid=(n,),
        in_specs=[resident((tm, D_MODEL)), resident((tm, DA_WIDTH)), resident((tm, ML_WIDTH)),
                  pl.BlockSpec((tm, D_MODEL), nxt), pl.BlockSpec((tm, DA_WIDTH), nxt),
                  pl.BlockSpec((tm, ML_WIDTH), nxt),
                  resident((DA_WIDTH, D_MODEL)), resident((ML_WIDTH, D_MODEL)),
                  resident((1, D_MODEL)),
                  resident((D_MODEL, 2 * D_FF)),
                  resident((FFN_CONV, 2 * D_FF)),
                  resident((1, 2 * D_FF)),
                  resident((D_FF, D_MODEL)),
                  resident((1, D_MODEL))],
        out_specs=pl.BlockSpec((tm, D_MODEL), row),
        out_shape=jax.ShapeDtypeStruct((T, D_MODEL), F32),
        scratch_shapes=[pltpu.VMEM((tm, D_MODEL), F32),
                        pltpu.VMEM((tm, D_MODEL), F32),
                        pltpu.VMEM((tm + HALO, D_MODEL), BF16),
                        pltpu.VMEM((2, 2, tm + 2 * HALO, FF_CHUNK), F32),
                        pltpu.VMEM((tm, D_FF), BF16)],
        compiler_params=_params("arbitrary"),
        name="mixer",
    )(x2, da, ml, x2, da, ml, wa, wm, nw, w_up, cw, cb, w_down, fw)


def _tile(n, pref):
    t = min(n, pref)
    assert n % t == 0, (n, t)
    return t


def kernel(x, attn_norm_w, w_in, mlstm_conv_w, mlstm_conv_b, mlstm_igate_b, mlstm_fgate_b, lambda_q1, lambda_k1, lambda_q2, lambda_k2, diff_norm_w, mlstm_norm_w, w_out, ffn_norm_w, w_up, ffn_conv_w, ffn_conv_b, w_down, final_norm_w):
    B, S, D = x.shape
    assert D == D_MODEL and S % (SUB_TILES * ATT_TILE) == 0 and attn_norm_w.shape[0] == 1
    T = B * S
    x2 = x.reshape(T, D)

    w = w_in[0]

    w_gate = jnp.pad(w[:, MAIN_COLS:], ((0, 0), (0, LANES - N_GATES)))
    q0, k0, v0, m0 = 0, DA_WIDTH, 2 * DA_WIDTH, 3 * DA_WIDTH
    w_row = jnp.concatenate([w[:, k0:v0], w[:, m0:MAIN_COLS], w_gate], axis=1).astype(BF16)
    w_t = jnp.concatenate([w[:, q0:k0], w[:, v0:m0]], axis=1).T.astype(BF16)
    augk, augq_t = _score_aug()
    qt, ka, vt, mqk, mv, mo, g = _inproj(x2, attn_norm_w, w_row, w_t, jnp.asarray(augk), jnp.asarray(augq_t),
                                         mlstm_conv_w[0], mlstm_conv_b, S)

    lam_vecs = jnp.concatenate([lambda_q1, lambda_k1, lambda_q2, lambda_k2], axis=0).astype(F32)
    da = _attn(qt, ka, vt, lam_vecs, diff_norm_w.reshape(DA_VDIM, 1), B, S)

    nchunks = T // ML_CHUNK
    g3 = g.reshape(nchunks, ML_CHUNK, N_GATES)
    gt3 = jnp.swapaxes(g3, 1, 2)
    gate_b = jnp.concatenate([mlstm_igate_b, mlstm_fgate_b], axis=1).astype(F32)
    ml = _mlstm(mqk, mv, mo, g3, gt3, gate_b, gate_b.reshape(N_GATES, 1), mlstm_norm_w, B, S, _tile(S, ML_STEP))

    w_o = w_out[0].astype(BF16)
    out = _mixer(x2, da, ml, w_o[:DA_WIDTH], w_o[DA_WIDTH:], ffn_norm_w, w_up[0].astype(BF16), ffn_conv_w[0],
                 ffn_conv_b, w_down[0].astype(BF16), final_norm_w.reshape(1, D), S, _tile(S, FF_ROWS))
    return out.reshape(B, S, D)
```

```python
import functools
import math

import jax
import jax.numpy as jnp
import numpy as np
from jax import lax
from jax.experimental import pallas as pl
from jax.experimental.pallas import tpu as pltpu

F32 = jnp.float32
BF16 = jnp.bfloat16

D_MODEL = 1024
DA_HEADS = 4
DA_VDIM = 128
DA_QKDIM = 64
DA_WIDTH = DA_HEADS * DA_VDIM
ML_HEADS = 4
ML_DIM = 128
ML_WIDTH = ML_HEADS * ML_DIM
ML_CHUNK = 64
ML_CONV = 4
FFN_CONV = 3
D_FF = 2816
NORM_EPS = 1e-6
LAM_INIT = 0.8 - 0.6 * math.exp(-0.3 * 0)
N_GATES = 2 * ML_HEADS
N_GROUPS = 2 * DA_HEADS
LOG2E = math.log2(math.e)
ALIBI_SLOPES_LOG2 = tuple(float(np.float32(2.0 ** (-8.0 * (h + 1) / DA_HEADS) * LOG2E)) for h in range(DA_HEADS))

LANES = 128
V_ROWS = DA_VDIM + 16
ATT_TILE = 512
HALO = 16
NEG_BIG = -1e30
VMEM_LIMIT = 56 * 1024 * 1024

MAIN_COLS = 2 * DA_WIDTH + DA_WIDTH + 2 * ML_WIDTH + ML_WIDTH + ML_WIDTH


def _params(*sem):
    return pltpu.CompilerParams(dimension_semantics=sem, vmem_limit_bytes=VMEM_LIMIT)


def _sigmoid(x):
    return 0.5 + 0.5 * jnp.tanh(0.5 * x)


def _rms(x, w):
    return x * lax.rsqrt(jnp.mean(x * x, axis=-1, keepdims=True) + NORM_EPS) * w


def _score_aug():
    r = np.arange(ATT_TILE)
    hi = ((r // 16) * 16).astype(np.float32)
    lo = (r % 16).astype(np.float32)
    augk = np.zeros((ATT_TILE, N_GROUPS * LANES), np.float32)
    augq_t = np.zeros((N_GROUPS * LANES, ATT_TILE), np.float32)
    for g in range(N_GROUPS):
        rest = np.float32(ALIBI_SLOPES_LOG2[g // 2])
        parts = []
        for _ in range(3):
            parts.append(np.asarray(rest).astype(BF16).astype(np.float32))
            rest = np.float32(rest - parts[-1])
        assert rest == 0.0
        c0 = g * LANES + (DA_QKDIM if g % 2 == 0 else 0)
        for i, ci in enumerate(parts):
            augk[:, c0 + i] = hi
            augk[:, c0 + 3 + i] = lo
            augk[:, c0 + 6 + i] = -ci
            augk[:, c0 + 9 + i] = -ci
            augq_t[c0 + i] = ci
            augq_t[c0 + 3 + i] = ci
            augq_t[c0 + 6 + i] = hi
            augq_t[c0 + 9 + i] = lo
    return augk, augq_t


SUB_TILES = 2


def _inproj_kernel(x_ref, xh_ref, nw_ref, wr_ref, wt_ref, augk_ref, augq_ref, cw_ref, cb_ref,
                   qt_ref, ka_ref, vt_ref, mqk_ref, mv_ref, mo_ref, g_ref, *, tiles_per_seq):
    i = pl.program_id(0)
    tm = ATT_TILE
    lane = lax.broadcasted_iota(jnp.int32, (tm, LANES), 1)
    one_row = (lax.broadcasted_iota(jnp.int32, (V_ROWS - DA_VDIM, tm), 0) == 0).astype(BF16)

    for sub in range(SUB_TILES):
        rows = slice(sub * tm, (sub + 1) * tm)
        hn = _rms(x_ref[rows, :], nw_ref[...]).astype(BF16)

        def mm(lo, hi):
            return jnp.dot(hn, wr_ref[:, lo:hi], preferred_element_type=F32)

        def mm_t(lo, hi):
            return lax.dot_general(wt_ref[lo:hi, :], hn, (((1,), (1,)), ((), ())), preferred_element_type=F32)

        qq = mm_t(0, DA_WIDTH) * (DA_QKDIM ** -0.5 * LOG2E)
        kk = mm(0, DA_WIDTH)
        vv = mm_t(DA_WIDTH, 2 * DA_WIDTH)
        for h in range(DA_HEADS):
            vt_ref[sub, h * V_ROWS:h * V_ROWS + DA_VDIM, :] = vv[h * DA_VDIM:(h + 1) * DA_VDIM, :].astype(BF16)
            vt_ref[sub, h * V_ROWS + DA_VDIM:(h + 1) * V_ROWS, :] = one_row
        for g in range(N_GROUPS):
            h, c = divmod(g, 2)
            src = h * LANES + c * DA_QKDIM
            dst = g * LANES + c * DA_QKDIM
            spare = g * LANES + (1 - c) * DA_QKDIM
            qt_ref[sub, dst:dst + DA_QKDIM, :] = qq[src:src + DA_QKDIM, :].astype(BF16)
            qt_ref[sub, spare:spare + DA_QKDIM, :] = augq_ref[spare:spare + DA_QKDIM, :].astype(BF16)
            is_data = (lane < DA_QKDIM) if c == 0 else (lane >= DA_QKDIM)
            ka_ref[rows, g * LANES:(g + 1) * LANES] = jnp.where(
                is_data, kk[:, h * LANES:(h + 1) * LANES], augk_ref[:, g * LANES:(g + 1) * LANES]).astype(BF16)

        c = DA_WIDTH
        xh = xh_ref[...] if sub == 0 else x_ref[sub * tm - HALO:sub * tm, :]
        hh = _rms(xh, nw_ref[...]).astype(BF16)
        first = ((i * SUB_TILES + sub) % tiles_per_seq == 0)
        for half in range(2):
            lo = c + half * ML_WIDTH
            hist = jnp.dot(hh, wr_ref[:, lo:lo + ML_WIDTH], preferred_element_type=F32)
            hist = jnp.where(first, 0.0, hist)
            pre = jnp.concatenate([hist, mm(lo, lo + ML_WIDTH)], axis=0)
            y = cb_ref[:, half * ML_WIDTH:(half + 1) * ML_WIDTH]
            for t in range(ML_CONV):
                off = HALO - (ML_CONV - 1) + t
                y = y + cw_ref[t:t + 1, half * ML_WIDTH:(half + 1) * ML_WIDTH] * pre[off:off + tm, :]
            y = y * _sigmoid(y)
            if half == 1:
                y = y * (ML_DIM ** -0.5)
            mqk_ref[rows, half * ML_WIDTH:(half + 1) * ML_WIDTH] = y.astype(BF16)
        c += 2 * ML_WIDTH
        mv_ref[rows, :] = mm(c, c + ML_WIDTH).astype(BF16); c += ML_WIDTH
        mo_ref[rows, :] = mm(c, c + ML_WIDTH).astype(BF16); c += ML_WIDTH
        cps = tm // ML_CHUNK
        g_ref[sub * cps:(sub + 1) * cps] = mm(c, c + LANES)[:, :N_GATES].reshape(cps, ML_CHUNK, N_GATES)


def _inproj(x2, nw, w_row, w_t, augk, augq_t, cw, cb, S):
    T = x2.shape[0]
    tm = ATT_TILE
    nt = T // tm
    rb = SUB_TILES * tm
    row = lambda i: (i, 0)
    row3 = lambda i: (i, 0, 0)
    halo = lambda i: (jnp.maximum(i * (rb // HALO) - 1, 0), 0)
    const = lambda i: (0, 0)
    resident = lambda shape: pl.BlockSpec(shape, const, pipeline_mode=pl.Buffered(1))
    bf = lambda n: jax.ShapeDtypeStruct((T, n), BF16)
    ga = N_GROUPS * LANES
    return pl.pallas_call(
        functools.partial(_inproj_kernel, tiles_per_seq=S // tm),
        grid=(nt // SUB_TILES,),
        in_specs=[pl.BlockSpec((rb, D_MODEL), row),
                  pl.BlockSpec((HALO, D_MODEL), halo),
                  resident((1, D_MODEL)),
                  resident(w_row.shape),
                  resident(w_t.shape),
                  resident((tm, ga)),
                  resident((ga, tm)),
                  resident((ML_CONV, 2 * ML_WIDTH)),
                  resident((1, 2 * ML_WIDTH))],
        out_specs=[pl.BlockSpec((SUB_TILES, ga, tm), row3), pl.BlockSpec((rb, ga), row),
                   pl.BlockSpec((SUB_TILES, DA_HEADS * V_ROWS, tm), row3), pl.BlockSpec((rb, 2 * ML_WIDTH), row),
                   pl.BlockSpec((rb, ML_WIDTH), row), pl.BlockSpec((rb, ML_WIDTH), row),
                   pl.BlockSpec((rb // ML_CHUNK, ML_CHUNK, N_GATES), row3)],
        out_shape=[jax.ShapeDtypeStruct((nt, ga, tm), BF16), bf(ga),
                   jax.ShapeDtypeStruct((nt, DA_HEADS * V_ROWS, tm), BF16), bf(2 * ML_WIDTH), bf(ML_WIDTH), bf(ML_WIDTH),
                   jax.ShapeDtypeStruct((T // ML_CHUNK, ML_CHUNK, N_GATES), F32)],
        compiler_params=_params("parallel"),
        name="inproj",
    )(x2, x2, nw, w_row, w_t, augk, augq_t, cw, cb)


def _attn_kernel(qt_ref, qn_ref, ka_ref, vt_ref, lam_ref, nw_ref, o_ref, s_sc, mt_sc, acc_sc, *, nq):
    t = ATT_TILE
    qi = pl.program_id(1)
    lv = lam_ref[...]
    lam = (jnp.exp(jnp.sum(lv[0:1] * lv[1:2], axis=-1, keepdims=True))
           - jnp.exp(jnp.sum(lv[2:3] * lv[3:4], axis=-1, keepdims=True)) + LAM_INIT)
    keep = lax.broadcasted_iota(jnp.int32, (t, t), 0) <= lax.broadcasted_iota(jnp.int32, (t, t), 1)

    def scores(h, ki, masked, q_ref=qt_ref):
        k0 = pl.multiple_of(ki * t, t)
        mts = []
        for c in range(2):
            g = 2 * h + c
            s = jnp.dot(ka_ref[pl.ds(k0, t), g * LANES:(g + 1) * LANES], q_ref[0, g * LANES:(g + 1) * LANES, :],
                        preferred_element_type=F32)
            if masked:
                s = jnp.where(keep, s, NEG_BIG)
            s_sc[g] = s
            mts.append(jnp.max(s, axis=0, keepdims=True))
        return mts

    def accumulate(h, ki, mts, stats):
        vt = vt_ref[ki, h * V_ROWS:(h + 1) * V_ROWS, :]
        shift = (ALIBI_SLOPES_LOG2[h] * t) * (ki - qi).astype(F32)
        out = []
        for c in range(2):
            g = 2 * h + c
            m_old = stats[c]
            m_new = jnp.maximum(m_old, mts[c] + shift)
            p = jnp.exp2(s_sc[g] - (m_new - shift))
            acc_sc[g] = jnp.exp2(m_old - m_new) * acc_sc[g] + jnp.dot(vt, p.astype(BF16), preferred_element_type=F32)
            out.append(m_new)
        return out

    def trip(ki, carry):
        prev = carry[0]
        out = [ki]
        for h in range(DA_HEADS):
            ch = carry[1 + 4 * h:5 + 4 * h]
            stats = accumulate(h, prev, ch[:2], ch[2:])
            out += scores(h, ki, False) + stats
        return tuple(out)

    @pl.when(qi == 0)
    def _():
        for h in range(DA_HEADS):
            mt_sc[2 * h], mt_sc[2 * h + 1] = scores(h, qi, True)

    acc_sc[...] = jnp.zeros(acc_sc.shape, F32)
    neg = jnp.full((1, t), NEG_BIG, F32)
    init = [qi]
    for h in range(DA_HEADS):
        init += [mt_sc[2 * h], mt_sc[2 * h + 1], neg, neg]
    carry = lax.fori_loop(0, qi // 2, lambda kp, cr: trip(2 * kp + 1, trip(2 * kp, cr)), tuple(init))
    carry = lax.cond(qi % 2 == 1, lambda cr: trip(qi - 1, cr), lambda cr: cr, carry)

    def last(with_next):
        for h in range(DA_HEADS):
            ch = carry[1 + 4 * h:5 + 4 * h]
            accumulate(h, carry[0], ch[:2], ch[2:])
            if with_next:
                mt_sc[2 * h], mt_sc[2 * h + 1] = scores(h, qi + 1, True, qn_ref)
            a0, a1 = acc_sc[2 * h], acc_sc[2 * h + 1]
            r0 = 1.0 / a0[DA_VDIM:DA_VDIM + 1]
            r1 = lam / a1[DA_VDIM:DA_VDIM + 1]
            o = a0[:DA_VDIM] * r0 - a1[:DA_VDIM] * r1
            o = o * lax.rsqrt(jnp.mean(o * o, axis=0, keepdims=True) + NORM_EPS) * nw_ref[...] * (1.0 - LAM_INIT)
            o_ref[:, h * DA_VDIM:(h + 1) * DA_VDIM] = o.T.astype(BF16)

    pl.when(qi < nq - 1)(lambda: last(True))
    pl.when(qi == nq - 1)(lambda: last(False))


def _attn(qt, ka, vt, lam_vecs, nw_col, B, S):
    t = ATT_TILE
    nq = S // t
    ga = N_GROUPS * LANES
    return pl.pallas_call(
        functools.partial(_attn_kernel, nq=nq),
        grid=(B, nq),
        in_specs=[pl.BlockSpec((1, ga, t), lambda b, i: (b * nq + i, 0, 0)),
                  pl.BlockSpec((1, ga, t), lambda b, i: (b * nq + jnp.minimum(i + 1, nq - 1), 0, 0)),
                  pl.BlockSpec((S, ga), lambda b, i: (b, 0)),
                  pl.BlockSpec((nq, DA_HEADS * V_ROWS, t), lambda b, i: (b, 0, 0)),
                  pl.BlockSpec((4, DA_QKDIM), lambda b, i: (0, 0)),
                  pl.BlockSpec((DA_VDIM, 1), lambda b, i: (0, 0))],
        out_specs=pl.BlockSpec((t, DA_WIDTH), lambda b, i: (b * nq + i, 0)),
        out_shape=jax.ShapeDtypeStruct((B * S, DA_WIDTH), BF16),
        scratch_shapes=[pltpu.VMEM((N_GROUPS, t, t), F32), pltpu.VMEM((N_GROUPS, 1, t), F32),
                        pltpu.VMEM((N_GROUPS, V_ROWS, t), F32)],
        compiler_params=_params("parallel", "arbitrary"),
        name="attn",
    )(qt, qt, ka, vt, lam_vecs, nw_col)


ML_STEP = 1024


def _log_sigmoid(x):
    return jnp.minimum(x, 0.0) - jnp.log(1.0 + jnp.exp(-jnp.abs(x)))


def _mlstm_kernel(qk_sc, v_ref, o_ref, g_ref, gt_ref, gbr_ref, gbc_ref,
                  nw_ref, out_ref, c_sc, m_sc, cprev_sc, mprev_sc, *, ts):
    j = pl.program_id(1)
    L = ML_CHUNK

    @pl.when(j == 0)
    def _():
        c_sc[...] = jnp.zeros(c_sc.shape, F32)
        m_sc[...] = jnp.zeros(m_sc.shape, F32)

    tt = lax.broadcasted_iota(jnp.int32, (L, L), 0)
    ss = lax.broadcasted_iota(jnp.int32, (L, L), 1)
    causal = ss <= tt
    nc = ts // L
    lane = lax.broadcasted_iota(jnp.int32, (nc, L, ML_DIM), 2)
    ones_col = jnp.where(lane == 0, 1.0, 0.0).astype(BF16)
    g = g_ref[...] + gbr_ref[...]
    gt = gt_ref[...] + gbc_ref[...]
    lf, lft = _log_sigmoid(g), _log_sigmoid(gt)
    bdot = lambda x, y, cx, cy: lax.dot_general(x, y, (((cx,), (cy,)), ((0,), (0,))), preferred_element_type=F32)

    for h in range(ML_HEADS):
        hs = slice(h * ML_DIM, (h + 1) * ML_DIM)
        i_col, f_col = g[:, :, h:h + 1], lf[:, :, ML_HEADS + h:ML_HEADS + h + 1]
        i_row, f_row = gt[:, h:h + 1, :], lft[:, ML_HEADS + h:ML_HEADS + h + 1, :]
        b_col = jnp.sum(jnp.where(causal, f_row, 0.0), axis=2, keepdims=True)
        b_row = jnp.sum(jnp.where(tt <= ss, f_col, 0.0), axis=1, keepdims=True)
        log_d = jnp.where(causal, b_col - b_row + i_row, NEG_BIG)
        a = jnp.max(log_d, axis=2, keepdims=True)
        d_loc = jnp.exp(log_d - a)

        q3 = qk_sc[:, hs].reshape(nc, L, ML_DIM)
        k3 = qk_sc[:, ML_WIDTH + h * ML_DIM:ML_WIDTH + (h + 1) * ML_DIM].reshape(nc, L, ML_DIM)
        v_aug = jnp.concatenate([v_ref[:, hs].reshape(nc, L, ML_DIM), ones_col], axis=2)

        sc = bdot(q3, k3, 2, 2) * d_loc
        intra = bdot(sc.astype(BF16), v_aug, 2, 1)
        b_last = b_col[:, L - 1:L, :]
        a_last = a[:, L - 1:L, :]
        w_loc = jnp.exp(b_last - b_col + i_col - a_last)
        wv = (w_loc * v_aug.astype(F32)).astype(BF16)
        upd = [lax.dot_general(k3[ci], wv[ci], (((0,), (0,)), ((), ())), preferred_element_type=F32)
               for ci in range(nc)]

        c_old = c_sc[h]
        m_old = m_sc[h]
        for ci in range(nc):
            cprev_sc[ci] = c_old.astype(BF16)
            mprev_sc[ci] = m_old
            m_new = jnp.maximum(b_last[ci] + m_old, a_last[ci])
            c_old = jnp.exp(b_last[ci] + m_old - m_new) * c_old + jnp.exp(a_last[ci] - m_new) * upd[ci]
            m_old = m_new
        c_sc[h] = c_old
        m_sc[h] = m_old

        log_prev = b_col + mprev_sc[...]
        m_t = jnp.maximum(log_prev, a)
        inter = bdot(q3, cprev_sc[...], 2, 1)
        tot = jnp.exp(a - m_t) * intra + jnp.exp(log_prev - m_t) * inter
        hh = tot[:, :, :ML_DIM] / jnp.maximum(jnp.abs(tot[:, :, ML_DIM:ML_DIM + 1]), jnp.exp(-m_t))
        hh = _rms(hh, nw_ref[:, hs]).reshape(ts, ML_DIM)
        gate = o_ref[:, hs].astype(F32)
        out_ref[:, hs] = (hh * _sigmoid(gate)).astype(BF16)


def _mlstm(mqk, mv, mo, g3, gt3, gb_row, gb_col, nw, B, S, ts):
    ns = S // ts
    cpb = ts // ML_CHUNK
    row = lambda b, j: (b * ns + j, 0)
    ch = lambda b, j: (b * ns + j, 0, 0)
    const = lambda b, j: (0, 0)
    return pl.pallas_call(
        functools.partial(_mlstm_kernel, ts=ts),
        grid=(B, ns),
        in_specs=[pl.BlockSpec((ts, 2 * ML_WIDTH), row),
                  pl.BlockSpec((ts, ML_WIDTH), row),
                  pl.BlockSpec((ts, ML_WIDTH), row),
                  pl.BlockSpec((cpb, ML_CHUNK, N_GATES), ch),
                  pl.BlockSpec((cpb, N_GATES, ML_CHUNK), ch),
                  pl.BlockSpec((1, N_GATES), const),
                  pl.BlockSpec((N_GATES, 1), const),
                  pl.BlockSpec((1, ML_WIDTH), const)],
        out_specs=pl.BlockSpec((ts, ML_WIDTH), row),
        out_shape=jax.ShapeDtypeStruct((B * S, ML_WIDTH), BF16),
        scratch_shapes=[pltpu.VMEM((ML_HEADS, ML_DIM, 2 * ML_DIM), F32),
                        pltpu.VMEM((ML_HEADS, 1, 1), F32),
                        pltpu.VMEM((cpb, ML_DIM, 2 * ML_DIM), BF16),
                        pltpu.VMEM((cpb, 1, 1), F32)],
        compiler_params=_params("parallel", "arbitrary"),
        name="mlstm",
    )(mqk, mv, mo, g3, gt3, gb_row, gb_col, nw)


FF_CHUNK = 256
FF_ROWS = 512


def _mixer_kernel(x0_ref, da0_ref, ml0_ref, xn_ref, dan_ref, mln_ref, wa_ref, wm_ref, nw_ref, wup_ref, cw_ref,
                  cb_ref, wd_ref, fw_ref, o_ref, x1_sc, x1n_sc, hn_sc, u_sc, act_sc, *, tm, tiles_per_seq):
    i = pl.program_id(0)

    def prepare(xr, dr, mr):
        x1 = (xr[...] + jnp.dot(dr[...], wa_ref[...], preferred_element_type=F32)
              + jnp.dot(mr[...], wm_ref[...], preferred_element_type=F32))
        x1n_sc[...] = x1
        hn_sc[HALO:, :] = _rms(x1, nw_ref[...]).astype(BF16)

    @pl.when(i == 0)
    def _():
        prepare(x0_ref, da0_ref, ml0_ref)
        hn_sc[0:HALO, :] = jnp.zeros((HALO, D_MODEL), BF16)

    x1_sc[...] = x1n_sc[...]
    nf = D_FF // FF_CHUNK

    def up(j):
        hn = hn_sc[...]
        for part in range(2):
            lo = part * D_FF + j * FF_CHUNK
            u_sc[j % 2, part] = jnp.dot(hn, wup_ref[:, lo:lo + FF_CHUNK], preferred_element_type=F32)

    def activate(j):
        y = []
        for part in range(2):
            lo = part * D_FF + j * FF_CHUNK
            acc = cb_ref[:, lo:lo + FF_CHUNK]
            for t in range(FFN_CONV):
                off = HALO - (FFN_CONV - 1) + t
                acc = acc + cw_ref[t:t + 1, lo:lo + FF_CHUNK] * u_sc[j % 2, part, off:off + tm, :]
            y.append(acc)
        act = y[0] * _sigmoid(y[0]) * y[1]
        act_sc[:, j * FF_CHUNK:(j + 1) * FF_CHUNK] = act.astype(BF16)

    up(0)
    for j in range(nf):
        if j + 1 < nf:
            up(j + 1)
        activate(j)

    hist = _rms(x1_sc[tm - HALO:tm, :], nw_ref[...])
    hn_sc[0:HALO, :] = jnp.where((i + 1) % tiles_per_seq == 0, 0.0, hist).astype(BF16)
    prepare(xn_ref, dan_ref, mln_ref)

    y = x1_sc[...] + jnp.dot(act_sc[...], wd_ref[...], preferred_element_type=F32)
    o_ref[...] = _rms(y, fw_ref[...])


def _mixer(x2, da, ml, wa, wm, nw, w_up, cw, cb, w_down, fw, S, tm):
    T = x2.shape[0]
    n = T // tm
    row = lambda i: (i, 0)
    nxt = lambda i: (jnp.minimum(i + 1, n - 1), 0)
    const = lambda i: (0, 0)
    resident = lambda shape: pl.BlockSpec(shape, const, pipeline_mode=pl.Buffered(1))
    return pl.pallas_call(
        functools.partial(_mixer_kernel, tm=tm, tiles_per_seq=S // tm),
        grid=(n,),
        in_specs=[resident((tm, D_MODEL)), resident((tm, DA_WIDTH)), resident((tm, ML_WIDTH)),
                  pl.BlockSpec((tm, D_MODEL), nxt), pl.BlockSpec((tm, DA_WIDTH), nxt),
                  pl.BlockSpec((tm, ML_WIDTH), nxt),
                  resident((DA_WIDTH, D_MODEL)), resident((ML_WIDTH, D_MODEL)),
                  resident((1, D_MODEL)),
                  resident((D_MODEL, 2 * D_FF)),
                  resident((FFN_CONV, 2 * D_FF)),
                  resident((1, 2 * D_FF)),
                  resident((D_FF, D_MODEL)),
                  resident((1, D_MODEL))],
        out_specs=pl.BlockSpec((tm, D_MODEL), row),
        out_shape=jax.ShapeDtypeStruct((T, D_MODEL), F32),
        scratch_shapes=[pltpu.VMEM((tm, D_MODEL), F32),
                        pltpu.VMEM((tm, D_MODEL), F32),
                        pltpu.VMEM((tm + HALO, D_MODEL), BF16),
                        pltpu.VMEM((2, 2, tm + HALO, FF_CHUNK), F32),
                        pltpu.VMEM((tm, D_FF), BF16)],
        compiler_params=_params("arbitrary"),
        name="mixer",
    )(x2, da, ml, x2, da, ml, wa, wm, nw, w_up, cw, cb, w_down, fw)


def _tile(n, pref):
    t = min(n, pref)
    assert n % t == 0, (n, t)
    return t


def kernel(x, attn_norm_w, w_in, mlstm_conv_w, mlstm_conv_b, mlstm_igate_b, mlstm_fgate_b, lambda_q1, lambda_k1, lambda_q2, lambda_k2, diff_norm_w, mlstm_norm_w, w_out, ffn_norm_w, w_up, ffn_conv_w, ffn_conv_b, w_down, final_norm_w):
    B, S, D = x.shape
    assert D == D_MODEL and S % (SUB_TILES * ATT_TILE) == 0 and attn_norm_w.shape[0] == 1
    T = B * S
    x2 = x.reshape(T, D)

    w = w_in[0]

    w_gate = jnp.pad(w[:, MAIN_COLS:], ((0, 0), (0, LANES - N_GATES)))
    q0, k0, v0, m0 = 0, DA_WIDTH, 2 * DA_WIDTH, 3 * DA_WIDTH
    w_row = jnp.concatenate([w[:, k0:v0], w[:, m0:MAIN_COLS], w_gate], axis=1).astype(BF16)
    w_t = jnp.concatenate([w[:, q0:k0], w[:, v0:m0]], axis=1).T.astype(BF16)
    augk, augq_t = _score_aug()
    qt, ka, vt, mqk, mv, mo, g = _inproj(x2, attn_norm_w, w_row, w_t, jnp.asarray(augk), jnp.asarray(augq_t),
                                         mlstm_conv_w[0], mlstm_conv_b, S)

    lam_vecs = jnp.concatenate([lambda_q1, lambda_k1, lambda_q2, lambda_k2], axis=0).astype(F32)
    da = _attn(qt, ka, vt, lam_vecs, diff_norm_w.reshape(DA_VDIM, 1), B, S)

    nchunks = T // ML_CHUNK
    g3 = g
    gt3 = jnp.swapaxes(g3, 1, 2)
    gate_b = jnp.concatenate([mlstm_igate_b, mlstm_fgate_b], axis=1).astype(F32)
    ml = _mlstm(mqk, mv, mo, g3, gt3, gate_b, gate_b.reshape(N_GATES, 1), mlstm_norm_w, B, S, _tile(S, ML_STEP))

    w_o = w_out[0].astype(BF16)
    out = _mixer(x2, da, ml, w_o[:DA_WIDTH], w_o[DA_WIDTH:], ffn_norm_w, w_up[0].astype(BF16), ffn_conv_w[0],
                 ffn_conv_b, w_down[0].astype(BF16), final_norm_w.reshape(1, D), S, _tile(S, FF_ROWS))
    return out.reshape(B, S, D)
```

```python
import functools
import math

import jax
import jax.numpy as jnp
import numpy as np
from jax import lax
from jax.experimental import pallas as pl
from jax.experimental.pallas import tpu as pltpu

F32 = jnp.float32
BF16 = jnp.bfloat16

D_MODEL = 1024
DA_HEADS = 4
DA_VDIM = 128
DA_QKDIM = 64
DA_WIDTH = DA_HEADS * DA_VDIM
ML_HEADS = 4
ML_DIM = 128
ML_WIDTH = ML_HEADS * ML_DIM
ML_CHUNK = 64
ML_CONV = 4
FFN_CONV = 3
D_FF = 2816
NORM_EPS = 1e-6
LAM_INIT = 0.8 - 0.6 * math.exp(-0.3 * 0)
N_GATES = 2 * ML_HEADS
N_GROUPS = 2 * DA_HEADS
LOG2E = math.log2(math.e)
ALIBI_SLOPES_LOG2 = tuple(float(np.float32(2.0 ** (-8.0 * (h + 1) / DA_HEADS) * LOG2E)) for h in range(DA_HEADS))

LANES = 128
V_ROWS = DA_VDIM + 16
ATT_TILE = 512
HALO = 16
NEG_BIG = -1e30
VMEM_LIMIT = 56 * 1024 * 1024

MAIN_COLS = 2 * DA_WIDTH + DA_WIDTH + 2 * ML_WIDTH + ML_WIDTH + ML_WIDTH


def _params(*sem):
    return pltpu.CompilerParams(dimension_semantics=sem, vmem_limit_bytes=VMEM_LIMIT)


def _sigmoid(x):
    return 0.5 + 0.5 * jnp.tanh(0.5 * x)


def _rms(x, w):
    return x * lax.rsqrt(jnp.mean(x * x, axis=-1, keepdims=True) + NORM_EPS) * w


def _score_aug():
    r = np.arange(ATT_TILE)
    hi = ((r // 16) * 16).astype(np.float32)
    lo = (r % 16).astype(np.float32)
    augk = np.zeros((ATT_TILE, N_GROUPS * LANES), np.float32)
    augq_t = np.zeros((N_GROUPS * LANES, ATT_TILE), np.float32)
    for g in range(N_GROUPS):
        rest = np.float32(ALIBI_SLOPES_LOG2[g // 2])
        parts = []
        for _ in range(3):
            parts.append(np.asarray(rest).astype(BF16).astype(np.float32))
            rest = np.float32(rest - parts[-1])
        assert rest == 0.0
        c0 = g * LANES + (DA_QKDIM if g % 2 == 0 else 0)
        for i, ci in enumerate(parts):
            augk[:, c0 + i] = hi
            augk[:, c0 + 3 + i] = lo
            augk[:, c0 + 6 + i] = -ci
            augk[:, c0 + 9 + i] = -ci
            augq_t[c0 + i] = ci
            augq_t[c0 + 3 + i] = ci
            augq_t[c0 + 6 + i] = hi
            augq_t[c0 + 9 + i] = lo
    return augk, augq_t


SUB_TILES = 2


def _inproj_kernel(x_ref, xh_ref, nw_ref, wr_ref, wt_ref, augk_ref, augq_ref, cw_ref, cb_ref,
                   qt_ref, ka_ref, vt_ref, mqk_ref, mv_ref, mo_ref, g_ref, gt_ref, *, tiles_per_seq):
    i = pl.program_id(0)
    tm = ATT_TILE
    lane = lax.broadcasted_iota(jnp.int32, (tm, LANES), 1)
    one_row = (lax.broadcasted_iota(jnp.int32, (V_ROWS - DA_VDIM, tm), 0) == 0).astype(BF16)

    for sub in range(SUB_TILES):
        rows = slice(sub * tm, (sub + 1) * tm)
        hn = _rms(x_ref[rows, :], nw_ref[...]).astype(BF16)

        def mm(lo, hi):
            return jnp.dot(hn, wr_ref[:, lo:hi], preferred_element_type=F32)

        def mm_t(lo, hi):
            return lax.dot_general(wt_ref[lo:hi, :], hn, (((1,), (1,)), ((), ())), preferred_element_type=F32)

        qq = mm_t(0, DA_WIDTH) * (DA_QKDIM ** -0.5 * LOG2E)
        kk = mm(0, DA_WIDTH)
        vv = mm_t(DA_WIDTH, 2 * DA_WIDTH)
        for h in range(DA_HEADS):
            vt_ref[sub, h * V_ROWS:h * V_ROWS + DA_VDIM, :] = vv[h * DA_VDIM:(h + 1) * DA_VDIM, :].astype(BF16)
            vt_ref[sub, h * V_ROWS + DA_VDIM:(h + 1) * V_ROWS, :] = one_row
        for g in range(N_GROUPS):
            h, c = divmod(g, 2)
            src = h * LANES + c * DA_QKDIM
            dst = g * LANES + c * DA_QKDIM
            spare = g * LANES + (1 - c) * DA_QKDIM
            qt_ref[sub, dst:dst + DA_QKDIM, :] = qq[src:src + DA_QKDIM, :].astype(BF16)
            qt_ref[sub, spare:spare + DA_QKDIM, :] = augq_ref[spare:spare + DA_QKDIM, :].astype(BF16)
            is_data = (lane < DA_QKDIM) if c == 0 else (lane >= DA_QKDIM)
            ka_ref[rows, g * LANES:(g + 1) * LANES] = jnp.where(
                is_data, kk[:, h * LANES:(h + 1) * LANES], augk_ref[:, g * LANES:(g + 1) * LANES]).astype(BF16)

        c = DA_WIDTH
        xh = xh_ref[...] if sub == 0 else x_ref[sub * tm - HALO:sub * tm, :]
        hh = _rms(xh, nw_ref[...]).astype(BF16)
        first = ((i * SUB_TILES + sub) % tiles_per_seq == 0)
        for half in range(2):
            lo = c + half * ML_WIDTH
            hist = jnp.dot(hh, wr_ref[:, lo:lo + ML_WIDTH], preferred_element_type=F32)
            hist = jnp.where(first, 0.0, hist)
            pre = jnp.concatenate([hist, mm(lo, lo + ML_WIDTH)], axis=0)
            y = cb_ref[:, half * ML_WIDTH:(half + 1) * ML_WIDTH]
            for t in range(ML_CONV):
                off = HALO - (ML_CONV - 1) + t
                y = y + cw_ref[t:t + 1, half * ML_WIDTH:(half + 1) * ML_WIDTH] * pre[off:off + tm, :]
            y = y * _sigmoid(y)
            if half == 1:
                y = y * (ML_DIM ** -0.5)
            mqk_ref[rows, half * ML_WIDTH:(half + 1) * ML_WIDTH] = y.astype(BF16)
        c += 2 * ML_WIDTH
        mv_ref[rows, :] = mm(c, c + ML_WIDTH).astype(BF16); c += ML_WIDTH
        mo_ref[rows, :] = mm(c, c + ML_WIDTH).astype(BF16); c += ML_WIDTH
        gates = mm(c, c + LANES)
        g_ref[rows, :] = gates[:, :N_GATES]
        gates_t = gates.T
        for ci in range(tm // ML_CHUNK):
            gt_ref[sub * (tm // ML_CHUNK) + ci] = gates_t[:N_GATES, ci * ML_CHUNK:(ci + 1) * ML_CHUNK]


def _inproj(x2, nw, w_row, w_t, augk, augq_t, cw, cb, S):
    T = x2.shape[0]
    tm = ATT_TILE
    nt = T // tm
    rb = SUB_TILES * tm
    row = lambda i: (i, 0)
    row3 = lambda i: (i, 0, 0)
    halo = lambda i: (jnp.maximum(i * (rb // HALO) - 1, 0), 0)
    const = lambda i: (0, 0)
    resident = lambda shape: pl.BlockSpec(shape, const, pipeline_mode=pl.Buffered(1))
    bf = lambda n: jax.ShapeDtypeStruct((T, n), BF16)
    ga = N_GROUPS * LANES
    return pl.pallas_call(
        functools.partial(_inproj_kernel, tiles_per_seq=S // tm),
        grid=(nt // SUB_TILES,),
        in_specs=[pl.BlockSpec((rb, D_MODEL), row),
                  pl.BlockSpec((HALO, D_MODEL), halo),
                  resident((1, D_MODEL)),
                  resident(w_row.shape),
                  resident(w_t.shape),
                  resident((tm, ga)),
                  resident((ga, tm)),
                  resident((ML_CONV, 2 * ML_WIDTH)),
                  resident((1, 2 * ML_WIDTH))],
        out_specs=[pl.BlockSpec((SUB_TILES, ga, tm), row3), pl.BlockSpec((rb, ga), row),
                   pl.BlockSpec((SUB_TILES, DA_HEADS * V_ROWS, tm), row3), pl.BlockSpec((rb, 2 * ML_WIDTH), row),
                   pl.BlockSpec((rb, ML_WIDTH), row), pl.BlockSpec((rb, ML_WIDTH), row),
                   pl.BlockSpec((rb, N_GATES), row),
                   pl.BlockSpec((rb // ML_CHUNK, N_GATES, ML_CHUNK), row3)],
        out_shape=[jax.ShapeDtypeStruct((nt, ga, tm), BF16), bf(ga),
                   jax.ShapeDtypeStruct((nt, DA_HEADS * V_ROWS, tm), BF16), bf(2 * ML_WIDTH), bf(ML_WIDTH), bf(ML_WIDTH),
                   jax.ShapeDtypeStruct((T, N_GATES), F32),
                   jax.ShapeDtypeStruct((T // ML_CHUNK, N_GATES, ML_CHUNK), F32)],
        compiler_params=_params("parallel"),
        name="inproj",
    )(x2, x2, nw, w_row, w_t, augk, augq_t, cw, cb)


def _attn_kernel(qt_ref, qn_ref, ka_ref, vt_ref, lam_ref, nw_ref, o_ref, s_sc, mt_sc, acc_sc, *, nq):
    t = ATT_TILE
    qi = pl.program_id(1)
    lv = lam_ref[...]
    lam = (jnp.exp(jnp.sum(lv[0:1] * lv[1:2], axis=-1, keepdims=True))
           - jnp.exp(jnp.sum(lv[2:3] * lv[3:4], axis=-1, keepdims=True)) + LAM_INIT)
    keep = lax.broadcasted_iota(jnp.int32, (t, t), 0) <= lax.broadcasted_iota(jnp.int32, (t, t), 1)

    def scores(h, ki, masked, q_ref=qt_ref):
        k0 = pl.multiple_of(ki * t, t)
        mts = []
        for c in range(2):
            g = 2 * h + c
            s = jnp.dot(ka_ref[pl.ds(k0, t), g * LANES:(g + 1) * LANES], q_ref[0, g * LANES:(g + 1) * LANES, :],
                        preferred_element_type=F32)
            if masked:
                s = jnp.where(keep, s, NEG_BIG)
            s_sc[g] = s
            mts.append(jnp.max(s, axis=0, keepdims=True))
        return mts

    def accumulate(h, ki, mts, stats):
        vt = vt_ref[ki, h * V_ROWS:(h + 1) * V_ROWS, :]
        shift = (ALIBI_SLOPES_LOG2[h] * t) * (ki - qi).astype(F32)
        out = []
        for c in range(2):
            g = 2 * h + c
            m_old = stats[c]
            m_new = jnp.maximum(m_old, mts[c] + shift)
            p = jnp.exp2(s_sc[g] - (m_new - shift))
            acc_sc[g] = jnp.exp2(m_old - m_new) * acc_sc[g] + jnp.dot(vt, p.astype(BF16), preferred_element_type=F32)
            out.append(m_new)
        return out

    def trip(ki, carry):
        prev = carry[0]
        out = [ki]
        for h in range(DA_HEADS):
            ch = carry[1 + 4 * h:5 + 4 * h]
            stats = accumulate(h, prev, ch[:2], ch[2:])
            out += scores(h, ki, False) + stats
        return tuple(out)

    @pl.when(qi == 0)
    def _():
        for h in range(DA_HEADS):
            mt_sc[2 * h], mt_sc[2 * h + 1] = scores(h, qi, True)

    acc_sc[...] = jnp.zeros(acc_sc.shape, F32)
    neg = jnp.full((1, t), NEG_BIG, F32)
    init = [qi]
    for h in range(DA_HEADS):
        init += [mt_sc[2 * h], mt_sc[2 * h + 1], neg, neg]
    carry = lax.fori_loop(0, qi // 2, lambda kp, cr: trip(2 * kp + 1, trip(2 * kp, cr)), tuple(init))
    carry = lax.cond(qi % 2 == 1, lambda cr: trip(qi - 1, cr), lambda cr: cr, carry)

    def last(with_next):
        for h in range(DA_HEADS):
            ch = carry[1 + 4 * h:5 + 4 * h]
            accumulate(h, carry[0], ch[:2], ch[2:])
            if with_next:
                mt_sc[2 * h], mt_sc[2 * h + 1] = scores(h, qi + 1, True, qn_ref)
            a0, a1 = acc_sc[2 * h], acc_sc[2 * h + 1]
            r0 = 1.0 / a0[DA_VDIM:DA_VDIM + 1]
            r1 = lam / a1[DA_VDIM:DA_VDIM + 1]
            o = a0[:DA_VDIM] * r0 - a1[:DA_VDIM] * r1
            o = o * lax.rsqrt(jnp.mean(o * o, axis=0, keepdims=True) + NORM_EPS) * nw_ref[...] * (1.0 - LAM_INIT)
            o_ref[:, h * DA_VDIM:(h + 1) * DA_VDIM] = o.T.astype(BF16)

    pl.when(qi < nq - 1)(lambda: last(True))
    pl.when(qi == nq - 1)(lambda: last(False))


def _attn(qt, ka, vt, lam_vecs, nw_col, B, S):
    t = ATT_TILE
    nq = S // t
    ga = N_GROUPS * LANES
    return pl.pallas_call(
        functools.partial(_attn_kernel, nq=nq),
        grid=(B, nq),
        in_specs=[pl.BlockSpec((1, ga, t), lambda b, i: (b * nq + i, 0, 0)),
                  pl.BlockSpec((1, ga, t), lambda b, i: (b * nq + jnp.minimum(i + 1, nq - 1), 0, 0)),
                  pl.BlockSpec((S, ga), lambda b, i: (b, 0)),
                  pl.BlockSpec((nq, DA_HEADS * V_ROWS, t), lambda b, i: (b, 0, 0)),
                  pl.BlockSpec((4, DA_QKDIM), lambda b, i: (0, 0)),
                  pl.BlockSpec((DA_VDIM, 1), lambda b, i: (0, 0))],
        out_specs=pl.BlockSpec((t, DA_WIDTH), lambda b, i: (b * nq + i, 0)),
        out_shape=jax.ShapeDtypeStruct((B * S, DA_WIDTH), BF16),
        scratch_shapes=[pltpu.VMEM((N_GROUPS, t, t), F32), pltpu.VMEM((N_GROUPS, 1, t), F32),
                        pltpu.VMEM((N_GROUPS, V_ROWS, t), F32)],
        compiler_params=_params("parallel", "arbitrary"),
        name="attn",
    )(qt, qt, ka, vt, lam_vecs, nw_col)


ML_STEP = 1024


def _log_sigmoid(x):
    return jnp.minimum(x, 0.0) - jnp.log(1.0 + jnp.exp(-jnp.abs(x)))


def _mlstm_kernel(qk_sc, v_ref, o_ref, g_ref, gt_ref, gbr_ref, gbc_ref,
                  nw_ref, out_ref, c_sc, m_sc, cprev_sc, mprev_sc, *, ts):
    j = pl.program_id(1)
    L = ML_CHUNK

    @pl.when(j == 0)
    def _():
        c_sc[...] = jnp.zeros(c_sc.shape, F32)
        m_sc[...] = jnp.zeros(m_sc.shape, F32)

    tt = lax.broadcasted_iota(jnp.int32, (L, L), 0)
    ss = lax.broadcasted_iota(jnp.int32, (L, L), 1)
    causal = ss <= tt
    nc = ts // L
    lane = lax.broadcasted_iota(jnp.int32, (nc, L, ML_DIM), 2)
    ones_col = jnp.where(lane == 0, 1.0, 0.0).astype(BF16)
    g = g_ref[...] + gbr_ref[...]
    gt = gt_ref[...] + gbc_ref[...]
    lf, lft = _log_sigmoid(g), _log_sigmoid(gt)
    bdot = lambda x, y, cx, cy: lax.dot_general(x, y, (((cx,), (cy,)), ((0,), (0,))), preferred_element_type=F32)

    for h in range(ML_HEADS):
        hs = slice(h * ML_DIM, (h + 1) * ML_DIM)
        i_col, f_col = g[:, :, h:h + 1], lf[:, :, ML_HEADS + h:ML_HEADS + h + 1]
        i_row, f_row = gt[:, h:h + 1, :], lft[:, ML_HEADS + h:ML_HEADS + h + 1, :]
        b_col = jnp.sum(jnp.where(causal, f_row, 0.0), axis=2, keepdims=True)
        b_row = jnp.sum(jnp.where(tt <= ss, f_col, 0.0), axis=1, keepdims=True)
        log_d = jnp.where(causal, b_col - b_row + i_row, NEG_BIG)
        a = jnp.max(log_d, axis=2, keepdims=True)
        d_loc = jnp.exp(log_d - a)

        q3 = qk_sc[:, hs].reshape(nc, L, ML_DIM)
        k3 = qk_sc[:, ML_WIDTH + h * ML_DIM:ML_WIDTH + (h + 1) * ML_DIM].reshape(nc, L, ML_DIM)
        v_aug = jnp.concatenate([v_ref[:, hs].reshape(nc, L, ML_DIM), ones_col], axis=2)

        sc = bdot(q3, k3, 2, 2) * d_loc
        intra = bdot(sc.astype(BF16), v_aug, 2, 1)
        b_last = b_col[:, L - 1:L, :]
        a_last = a[:, L - 1:L, :]
        w_loc = jnp.exp(b_last - b_col + i_col - a_last)
        wv = (w_loc * v_aug.astype(F32)).astype(BF16)
        upd = [lax.dot_general(k3[ci], wv[ci], (((0,), (0,)), ((), ())), preferred_element_type=F32)
               for ci in range(nc)]

        c_old = c_sc[h]
        m_old = m_sc[h]
        for ci in range(nc):
            cprev_sc[ci] = c_old.astype(BF16)
            mprev_sc[ci] = m_old
            m_new = jnp.maximum(b_last[ci] + m_old, a_last[ci])
            c_old = jnp.exp(b_last[ci] + m_old - m_new) * c_old + jnp.exp(a_last[ci] - m_new) * upd[ci]
            m_old = m_new
        c_sc[h] = c_old
        m_sc[h] = m_old

        log_prev = b_col + mprev_sc[...]
        m_t = jnp.maximum(log_prev, a)
        inter = bdot(q3, cprev_sc[...], 2, 1)
        tot = jnp.exp(a - m_t) * intra + jnp.exp(log_prev - m_t) * inter
        hh = tot[:, :, :ML_DIM] / jnp.maximum(jnp.abs(tot[:, :, ML_DIM:ML_DIM + 1]), jnp.exp(-m_t))
        hh = _rms(hh, nw_ref[:, hs]).reshape(ts, ML_DIM)
        gate = o_ref[:, hs].astype(F32)
        out_ref[:, hs] = (hh * _sigmoid(gate)).astype(BF16)


def _mlstm(mqk, mv, mo, g3, gt3, gb_row, gb_col, nw, B, S, ts):
    ns = S // ts
    cpb = ts // ML_CHUNK
    row = lambda b, j: (b * ns + j, 0)
    ch = lambda b, j: (b * ns + j, 0, 0)
    const = lambda b, j: (0, 0)
    return pl.pallas_call(
        functools.partial(_mlstm_kernel, ts=ts),
        grid=(B, ns),
        in_specs=[pl.BlockSpec((ts, 2 * ML_WIDTH), row),
                  pl.BlockSpec((ts, ML_WIDTH), row),
                  pl.BlockSpec((ts, ML_WIDTH), row),
                  pl.BlockSpec((cpb, ML_CHUNK, N_GATES), ch),
                  pl.BlockSpec((cpb, N_GATES, ML_CHUNK), ch),
                  pl.BlockSpec((1, N_GATES), const),
                  pl.BlockSpec((N_GATES, 1), const),
                  pl.BlockSpec((1, ML_WIDTH), const)],
        out_specs=pl.BlockSpec((ts, ML_WIDTH), row),
        out_shape=jax.ShapeDtypeStruct((B * S, ML_WIDTH), BF16),
        scratch_shapes=[pltpu.VMEM((ML_HEADS, ML_DIM, 2 * ML_DIM), F32),
                        pltpu.VMEM((ML_HEADS, 1, 1), F32),
                        pltpu.VMEM((cpb, ML_DIM, 2 * ML_DIM), BF16),
                        pltpu.VMEM((cpb, 1, 1), F32)],
        compiler_params=_params("parallel", "arbitrary"),
        name="mlstm",
    )(mqk, mv, mo, g3, gt3, gb_row, gb_col, nw)


FF_CHUNK = 256
FF_ROWS = 512


def _mixer_kernel(x0_ref, da0_ref, ml0_ref, xn_ref, dan_ref, mln_ref, wa_ref, wm_ref, nw_ref, wup_ref, cw_ref,
                  cb_ref, wd_ref, fw_ref, o_ref, x1_sc, x1n_sc, hn_sc, u_sc, act_sc, *, tm, tiles_per_seq):
    i = pl.program_id(0)

    def prepare(xr, dr, mr):
        x1 = (xr[...] + jnp.dot(dr[...], wa_ref[...], preferred_element_type=F32)
              + jnp.dot(mr[...], wm_ref[...], preferred_element_type=F32))
        x1n_sc[...] = x1
        hn_sc[HALO:, :] = _rms(x1, nw_ref[...]).astype(BF16)

    @pl.when(i == 0)
    def _():
        prepare(x0_ref, da0_ref, ml0_ref)
        hn_sc[0:HALO, :] = jnp.zeros((HALO, D_MODEL), BF16)

    x1_sc[...] = x1n_sc[...]
    nf = D_FF // FF_CHUNK

    def up(j):
        hn = hn_sc[...]
        for part in range(2):
            lo = part * D_FF + j * FF_CHUNK
            u_sc[j % 2, part] = jnp.dot(hn, wup_ref[:, lo:lo + FF_CHUNK], preferred_element_type=F32)

    def activate(j):
        y = []
        for part in range(2):
            lo = part * D_FF + j * FF_CHUNK
            acc = cb_ref[:, lo:lo + FF_CHUNK]
            for t in range(FFN_CONV):
                off = HALO - (FFN_CONV - 1) + t
                acc = acc + cw_ref[t:t + 1, lo:lo + FF_CHUNK] * u_sc[j % 2, part, off:off + tm, :]
            y.append(acc)
        act = y[0] * _sigmoid(y[0]) * y[1]
        act_sc[:, j * FF_CHUNK:(j + 1) * FF_CHUNK] = act.astype(BF16)

    up(0)
    for j in range(nf):
        if j + 1 < nf:
            up(j + 1)
        activate(j)

    hist = _rms(x1_sc[tm - HALO:tm, :], nw_ref[...])
    hn_sc[0:HALO, :] = jnp.where((i + 1) % tiles_per_seq == 0, 0.0, hist).astype(BF16)
    prepare(xn_ref, dan_ref, mln_ref)

    y = x1_sc[...] + jnp.dot(act_sc[...], wd_ref[...], preferred_element_type=F32)
    o_ref[...] = _rms(y, fw_ref[...])


def _mixer(x2, da, ml, wa, wm, nw, w_up, cw, cb, w_down, fw, S, tm):
    T = x2.shape[0]
    n = T // tm
    row = lambda i: (i, 0)
    nxt = lambda i: (jnp.minimum(i + 1, n - 1), 0)
    const = lambda i: (0, 0)
    resident = lambda shape: pl.BlockSpec(shape, const, pipeline_mode=pl.Buffered(1))
    return pl.pallas_call(
        functools.partial(_mixer_kernel, tm=tm, tiles_per_seq=S // tm),
        grid=(n,),
        in_specs=[resident((tm, D_MODEL)), resident((tm, DA_WIDTH)), resident((tm, ML_WIDTH)),
                  pl.BlockSpec((tm, D_MODEL), nxt), pl.BlockSpec((tm, DA_WIDTH), nxt),
                  pl.BlockSpec((tm, ML_WIDTH), nxt),
                  resident((DA_WIDTH, D_MODEL)), resident((ML_WIDTH, D_MODEL)),
                  resident((1, D_MODEL)),
                  resident((D_MODEL, 2 * D_FF)),
                  resident((FFN_CONV, 2 * D_FF)),
                  resident((1, 2 * D_FF)),
                  resident((D_FF, D_MODEL)),
                  resident((1, D_MODEL))],
        out_specs=pl.BlockSpec((tm, D_MODEL), row),
        out_shape=jax.ShapeDtypeStruct((T, D_MODEL), F32),
        scratch_shapes=[pltpu.VMEM((tm, D_MODEL), F32),
                        pltpu.VMEM((tm, D_MODEL), F32),
                        pltpu.VMEM((tm + HALO, D_MODEL), BF16),
                        pltpu.VMEM((2, 2, tm + HALO, FF_CHUNK), F32),
                        pltpu.VMEM((tm, D_FF), BF16)],
        compiler_params=_params("arbitrary"),
        name="mixer",
    )(x2, da, ml, x2, da, ml, wa, wm, nw, w_up, cw, cb, w_down, fw)


def _tile(n, pref):
    t = min(n, pref)
    assert n % t == 0, (n, t)
    return t


def kernel(x, attn_norm_w, w_in, mlstm_conv_w, mlstm_conv_b, mlstm_igate_b, mlstm_fgate_b, lambda_q1, lambda_k1, lambda_q2, lambda_k2, diff_norm_w, mlstm_norm_w, w_out, ffn_norm_w, w_up, ffn_conv_w, ffn_conv_b, w_down, final_norm_w):
    B, S, D = x.shape
    assert D == D_MODEL and S % (SUB_TILES * ATT_TILE) == 0 and attn_norm_w.shape[0] == 1
    T = B * S
    x2 = x.reshape(T, D)

    w = w_in[0]

    w_gate = jnp.pad(w[:, MAIN_COLS:], ((0, 0), (0, LANES - N_GATES)))
    q0, k0, v0, m0 = 0, DA_WIDTH, 2 * DA_WIDTH, 3 * DA_WIDTH
    w_row = jnp.concatenate([w[:, k0:v0], w[:, m0:MAIN_COLS], w_gate], axis=1).astype(BF16)
    w_t = jnp.concatenate([w[:, q0:k0], w[:, v0:m0]], axis=1).T.astype(BF16)
    augk, augq_t = _score_aug()
    qt, ka, vt, mqk, mv, mo, g, gt3 = _inproj(x2, attn_norm_w, w_row, w_t, jnp.asarray(augk), jnp.asarray(augq_t),
                                         mlstm_conv_w[0], mlstm_conv_b, S)

    lam_vecs = jnp.concatenate([lambda_q1, lambda_k1, lambda_q2, lambda_k2], axis=0).astype(F32)
    da = _attn(qt, ka, vt, lam_vecs, diff_norm_w.reshape(DA_VDIM, 1), B, S)

    nchunks = T // ML_CHUNK
    g3 = g.reshape(nchunks, ML_CHUNK, N_GATES)
    gate_b = jnp.concatenate([mlstm_igate_b, mlstm_fgate_b], axis=1).astype(F32)
    ml = _mlstm(mqk, mv, mo, g3, gt3, gate_b, gate_b.reshape(N_GATES, 1), mlstm_norm_w, B, S, _tile(S, ML_STEP))

    w_o = w_out[0].astype(BF16)
    out = _mixer(x2, da, ml, w_o[:DA_WIDTH], w_o[DA_WIDTH:], ffn_norm_w, w_up[0].astype(BF16), ffn_conv_w[0],
                 ffn_conv_b, w_down[0].astype(BF16), final_norm_w.reshape(1, D), S, _tile(S, FF_ROWS))
    return out.reshape(B, S, D)
```
